```python
import math
import jax, jax.numpy as jnp
from jax import lax
import numpy as np

D_MODEL = 4096
BATCH = 4
SEQ = 2048
DEPTH = 1
DEC_BATCH = 1
DEC_SEQ = 16384
PAST_LEN = 128

GRID_W = 64
NA_HEADS = 16
NA_HEAD_DIM = 128
NA_WIDTH = NA_HEADS * NA_HEAD_DIM
NA_KH = 8
NA_KW = 16
NA_COL_BLOCK = 16
NA_KEY_BLOCK = NA_COL_BLOCK + NA_KW
MLA_HEADS = 16
Q_LORA = 896
KV_LORA = 512
QK_NOPE = 128
QK_ROPE = 64
V_HEAD = 128
ROPE_THETA = 10000.0
Q_BLOCK = 128
D_FF = 11008
CONV_W = 3
NORM_EPS = 1e-6
N_MOD = 6
IN_SIZES = (NA_WIDTH, NA_WIDTH, NA_WIDTH, Q_LORA, KV_LORA, QK_ROPE, D_MODEL, D_MODEL)
IN_WIDTH = 3 * NA_WIDTH + Q_LORA + KV_LORA + QK_ROPE + 2 * D_MODEL

kernel_name = 'hybrid_na_mla_convglu_encoder'


def _rms_norm(x, g):
    xf = x.astype(jnp.float32)
    y = xf * lax.rsqrt(jnp.mean(xf * xf, axis=-1, keepdims=True) + NORM_EPS)
    return (y * g.astype(jnp.float32)).astype(x.dtype)


def _rope_tables(L, dtype):
    inv = jnp.power(ROPE_THETA, -jnp.arange(0, QK_ROPE, 2, dtype=jnp.float32) / QK_ROPE)
    ang = jnp.arange(L, dtype=jnp.float32)[:, None] * inv[None, :]
    return jnp.cos(ang).astype(dtype), jnp.sin(ang).astype(dtype)


def _rope(x, cos, sin):
    half = QK_ROPE // 2
    x1, x2 = x[..., :half], x[..., half:]
    return jnp.concatenate([x1 * cos - x2 * sin, x1 * sin + x2 * cos], axis=-1)


def _neighborhood_attention(q, k, v, rpb):
    b, L, h, d = q.shape
    rows = L // GRID_W
    kh = min(NA_KH, rows)
    n_cb = GRID_W // NA_COL_BLOCK
    qcol = np.arange(GRID_W).reshape(n_cb, NA_COL_BLOCK)
    kstart = np.clip(np.arange(n_cb) * NA_COL_BLOCK - NA_KW // 2, 0, GRID_W - NA_KEY_BLOCK)
    kcol = kstart[:, None] + np.arange(NA_KEY_BLOCK)[None, :]
    cstart = np.clip(qcol - NA_KW // 2, 0, GRID_W - NA_KW)
    dc = kcol[:, None, :] - qcol[:, :, None]
    col_valid = (kcol[:, None, :] >= cstart[:, :, None]) & (kcol[:, None, :] < cstart[:, :, None] + NA_KW)
    dc_idx = np.clip(dc + NA_KW - 1, 0, 2 * NA_KW - 2)
    mask = jnp.asarray(col_valid)[:, :, None, :]
    qg = jnp.moveaxis(q.reshape(b, rows, n_cb, NA_COL_BLOCK, h, d), 1, 0)
    kcb = k.reshape(b, rows, GRID_W, h, d)[:, :, kcol]
    vcb = v.reshape(b, rows, GRID_W, h, d)[:, :, kcol]
    scale = d ** -0.5

    def row_fn(args):
        q_row, r = args
        start = jnp.clip(r - kh // 2, 0, rows - kh)
        k_row = lax.dynamic_slice_in_dim(kcb, start, kh, axis=1)
        v_row = lax.dynamic_slice_in_dim(vcb, start, kh, axis=1)
        s = jnp.einsum('bnqhd,bknjhd->bhnqkj', q_row, k_row).astype(jnp.float32) * scale
        dr_idx = start + jnp.arange(kh) - r + NA_KH - 1
        bias = rpb[:, dr_idx][:, :, dc_idx]
        s = s + jnp.transpose(bias, (0, 2, 3, 1, 4)).astype(jnp.float32)
        s = jnp.where(mask, s, jnp.float32(-1e30))
        p = jax.nn.softmax(s.reshape(s.shape[:4] + (kh * NA_KEY_BLOCK,)), axis=-1).reshape(s.shape)
        return jnp.einsum('bhnqkj,bknjhd->bnqhd', p.astype(v_row.dtype), v_row)

    out = lax.map(row_fn, (qg, jnp.arange(rows)))
    return jnp.moveaxis(out, 0, 1).reshape(b, L, h * d)


def _mla(q_down, kv_down, k_rope, g_q, w_uq, g_kv, w_ukv):
    b, L, _ = q_down.shape
    q = (_rms_norm(q_down, g_q) @ w_uq).reshape(b, L, MLA_HEADS, QK_NOPE + QK_ROPE)
    kv = (_rms_norm(kv_down, g_kv) @ w_ukv).reshape(b, L, MLA_HEADS, QK_NOPE + V_HEAD)
    q_nope, q_pe = q[..., :QK_NOPE], q[..., QK_NOPE:]
    k_nope, v = kv[..., :QK_NOPE], kv[..., QK_NOPE:]
    cos, sin = _rope_tables(L, q.dtype)
    q_pe = _rope(q_pe, cos[:, None, :], sin[:, None, :])
    k_pe = _rope(k_rope, cos, sin)
    scale = (QK_NOPE + QK_ROPE) ** -0.5
    nq = L // Q_BLOCK
    qn = jnp.moveaxis(q_nope.reshape(b, nq, Q_BLOCK, MLA_HEADS, QK_NOPE), 1, 0)
    qp = jnp.moveaxis(q_pe.reshape(b, nq, Q_BLOCK, MLA_HEADS, QK_ROPE), 1, 0)

    def block_fn(args):
        qn_b, qp_b = args
        s = (jnp.einsum('bqhd,bkhd->bhqk', qn_b, k_nope)
             + jnp.einsum('bqhr,bkr->bhqk', qp_b, k_pe)).astype(jnp.float32) * scale
        p = jax.nn.softmax(s, axis=-1)
        return jnp.einsum('bhqk,bkhd->bqhd', p.astype(v.dtype), v)

    out = lax.map(block_fn, (qn, qp))
    return jnp.moveaxis(out, 0, 1).reshape(b, L, MLA_HEADS * V_HEAD)


def _layer(x, c, w_ada, b_ada, g_pre_mix, g_post_mix, w_in, rpb, g_q, w_uq, g_kv, w_ukv,
           w_na_proj, w_mla_proj, w_out, g_pre_ffn, g_post_ffn, w_ffn_in, conv_w, conv_b, w_ffn_down):
    b, L, _ = x.shape
    mod = jax.nn.silu(c) @ w_ada + b_ada
    shift1, scale1, gate1, shift2, scale2, gate2 = jnp.split(mod[:, None, :], N_MOD, axis=-1)
    h = _rms_norm(x, g_pre_mix) * (1.0 + scale1) + shift1
    z = h @ w_in
    cuts = [int(s) for s in np.cumsum(IN_SIZES)[:-1]]
    q_na, k_na, v_na, q_down, kv_down, k_rope, gate_na, gate_mla = jnp.split(z, cuts, axis=-1)
    hs = (b, L, NA_HEADS, NA_HEAD_DIM)
    o_na = _neighborhood_attention(q_na.reshape(hs), k_na.reshape(hs), v_na.reshape(hs), rpb)
    o_mla = _mla(q_down, kv_down, k_rope, g_q, w_uq, g_kv, w_ukv)
    merged = jax.nn.sigmoid(gate_na) * (o_na @ w_na_proj) + jax.nn.sigmoid(gate_mla) * (o_mla @ w_mla_proj)
    x = x + gate1 * _rms_norm(merged @ w_out, g_post_mix)
    h = _rms_norm(x, g_pre_ffn) * (1.0 + scale2) + shift2
    a, u = jnp.split(h @ w_ffn_in, 2, axis=-1)
    ap = jnp.pad(a, ((0, 0), (1, 1), (0, 0)))
    a = ap[:, :-2] * conv_w[0] + ap[:, 1:-1] * conv_w[1] + ap[:, 2:] * conv_w[2] + conv_b
    f = (jax.nn.gelu(a) * u) @ w_ffn_down
    return x + gate2 * _rms_norm(f, g_post_ffn)


def setup_inputs(seed: int = 0) -> dict:
    key = jax.random.key(seed)
    ks = jax.random.split(key, 24)

    def nrm(k, shape, scale):
        return jax.random.normal(k, shape, jnp.float32) * scale

    def gain(k, shape):
        return 1.0 + 0.05 * jax.random.normal(k, shape, jnp.float32)

    D = D_MODEL
    return {
        'x_prompt': nrm(ks[0], (BATCH, SEQ, D), 1.0),
        'x_sample': nrm(ks[1], (DEC_BATCH, DEC_SEQ, D), 1.0),
        'c_prompt': nrm(ks[2], (BATCH, D), 1.0),
        'c_sample': nrm(ks[3], (DEC_BATCH, D), 1.0),
        'w_ada': nrm(ks[4], (DEPTH, D, N_MOD * D), 0.5 * D ** -0.5),
        'b_ada': nrm(ks[5], (DEPTH, N_MOD * D), 0.02),
        'g_pre_mix': gain(ks[6], (DEPTH, D)),
        'g_post_mix': gain(ks[7], (DEPTH, D)),
        'w_in': nrm(ks[8], (DEPTH, D, IN_WIDTH), D ** -0.5),
        'rpb': nrm(ks[9], (DEPTH, NA_HEADS, 2 * NA_KH - 1, 2 * NA_KW - 1), 0.1),
        'g_q': gain(ks[10], (DEPTH, Q_LORA)),
        'w_uq': nrm(ks[11], (DEPTH, Q_LORA, MLA_HEADS * (QK_NOPE + QK_ROPE)), Q_LORA ** -0.5),
        'g_kv': gain(ks[12], (DEPTH, KV_LORA)),
        'w_ukv': nrm(ks[13], (DEPTH, KV_LORA, MLA_HEADS * (QK_NOPE + V_HEAD)), KV_LORA ** -0.5),
        'w_na_proj': nrm(ks[14], (DEPTH, NA_WIDTH, D), NA_WIDTH ** -0.5),
        'w_mla_proj': nrm(ks[15], (DEPTH, MLA_HEADS * V_HEAD, D), (MLA_HEADS * V_HEAD) ** -0.5),
        'w_out': nrm(ks[16], (DEPTH, D, D), D ** -0.5),
        'g_pre_ffn': gain(ks[17], (DEPTH, D)),
        'g_post_ffn': gain(ks[18], (DEPTH, D)),
        'w_ffn_in': nrm(ks[19], (DEPTH, D, 2 * D_FF), D ** -0.5),
        'conv_w': nrm(ks[20], (DEPTH, CONV_W, D_FF), CONV_W ** -0.5),
        'conv_b': nrm(ks[21], (DEPTH, D_FF), 0.01),
        'w_ffn_down': nrm(ks[22], (DEPTH, D_FF, D), D_FF ** -0.5),
    }


def reference(x_prompt, x_sample, c_prompt, c_sample, w_ada, b_ada, g_pre_mix, g_post_mix, w_in, rpb,
              g_q, w_uq, g_kv, w_ukv, w_na_proj, w_mla_proj, w_out, g_pre_ffn, g_post_ffn,
              w_ffn_in, conv_w, conv_b, w_ffn_down):
    y_prompt = x_prompt
    y_sample = x_sample
    for l in range(DEPTH):
        params = (w_ada[l], b_ada[l], g_pre_mix[l], g_post_mix[l], w_in[l], rpb[l], g_q[l], w_uq[l],
                  g_kv[l], w_ukv[l], w_na_proj[l], w_mla_proj[l], w_out[l], g_pre_ffn[l], g_post_ffn[l],
                  w_ffn_in[l], conv_w[l], conv_b[l], w_ffn_down[l])
        y_prompt = _layer(y_prompt, c_prompt, *params)
        y_sample = _layer(y_sample, c_sample, *params)
    return (y_prompt, y_sample)
```

```python
import functools

import numpy as np
import jax
import jax.numpy as jnp
from jax import lax
from jax.experimental import pallas as pl
from jax.experimental.pallas import tpu as pltpu

F32 = jnp.float32
BF16 = jnp.bfloat16

D = 4096
B_P, L_P = 4, 2048
L_S = 16384
T_S = L_S
T_P = B_P * L_P
T = T_S + T_P
N_SEQ = 1 + B_P
N_MOD = 6
GRID_W = 64
NA_HEADS, NA_DIM = 16, 128
NA_WIDTH = NA_HEADS * NA_DIM
NA_KH, NA_KW = 8, 16
MLA_HEADS = 16
Q_LORA, KV_LORA = 896, 512
QK_NOPE, QK_ROPE, V_HEAD = 128, 64, 128
ROPE_THETA = 10000.0
D_FF = 11008
D_FF_PAD = 11264
EPS = 1e-6
NEG = -1e30

LAT_W = 2048
LAT_Q, LAT_KV, LAT_ROPE = 0, 1024, 1536

V7X_VMEM_BYTES = 64 * 1024 * 1024
VMEM_LIMIT = V7X_VMEM_BYTES - 8 * 1024 * 1024
LANES = 128
BF16_SUBLANES = 16

NA_QROWS = 8
NA_SLAB_ROWS = 16
NA_BQ = NA_QROWS * GRID_W
NA_BK = NA_SLAB_ROWS * GRID_W


def _params(n_axes):
    return pltpu.CompilerParams(dimension_semantics=("arbitrary",) * n_axes,
                                vmem_limit_bytes=VMEM_LIMIT)


def _seq_of_block(i, bm):
    t0 = i * bm
    return jnp.where(t0 < T_S, 0, 1 + jnp.maximum(t0 - T_S, 0) // L_P)


def _ada_kernel(c_ref, w_ref, b_ref, o_ref):
    c = c_ref[...]
    s = c * jax.nn.sigmoid(c)
    o_ref[...] = jnp.dot(s.astype(BF16), w_ref[...].astype(BF16),
                         preferred_element_type=F32) + b_ref[...]


def _ada(c8, w_ada, b_ada):
    bn = 512
    n = N_MOD * D
    return pl.pallas_call(
        _ada_kernel,
        grid=(n // bn,),
        in_specs=[pl.BlockSpec((8, D), lambda j: (0, 0)),
                  pl.BlockSpec((D, bn), lambda j: (0, j)),
                  pl.BlockSpec((1, bn), lambda j: (0, j))],
        out_specs=pl.BlockSpec((8, bn), lambda j: (0, j)),
        out_shape=jax.ShapeDtypeStruct((8, n), F32),
        compiler_params=_params(1),
        name="ada_mod",
    )(c8, w_ada, b_ada.reshape(1, n))


def _norm_mod_kernel(x_ref, g_ref, mod_ref, o_ref, *, shift_idx, scale_idx):
    x = x_ref[...]
    ms = jnp.mean(x * x, axis=-1, keepdims=True)
    y = x * lax.rsqrt(ms + EPS) * g_ref[...]
    shift = mod_ref[0, shift_idx:shift_idx + 1, :]
    scale = mod_ref[0, scale_idx:scale_idx + 1, :]
    o_ref[...] = (y * (1.0 + scale) + shift).astype(BF16)


def _norm_mod(x, g, mod3, shift_idx, scale_idx):
    bm = 256
    return pl.pallas_call(
        functools.partial(_norm_mod_kernel, shift_idx=shift_idx, scale_idx=scale_idx),
        grid=(T // bm,),
        in_specs=[pl.BlockSpec((bm, D), lambda i: (i, 0)),
                  pl.BlockSpec((1, D), lambda i: (0, 0)),
                  pl.BlockSpec((1, N_MOD, D), lambda i: (_seq_of_block(i, bm), 0, 0))],
        out_specs=pl.BlockSpec((bm, D), lambda i: (i, 0)),
        out_shape=jax.ShapeDtypeStruct((T, D), BF16),
        compiler_params=_params(1),
        name="norm_mod",
    )(x, g.reshape(1, D), mod3)


def _mm_kernel(a_ref, w_ref, o_ref):
    o_ref[...] = jnp.dot(a_ref[...], w_ref[...],
                         preferred_element_type=F32).astype(o_ref.dtype)


def _matmul(a, w, out_dtype, bm, bn, name):
    m, k = a.shape
    _, n = w.shape
    return pl.pallas_call(
        _mm_kernel,
        grid=(m // bm, n // bn),
        in_specs=[pl.BlockSpec((bm, k), lambda i, j: (i, 0)),
                  pl.BlockSpec((k, bn), lambda i, j: (0, j))],
        out_specs=pl.BlockSpec((bm, bn), lambda i, j: (i, j)),
        out_shape=jax.ShapeDtypeStruct((m, n), out_dtype),
        compiler_params=_params(2),
        name=name,
    )(a, w)


def _na_case_geometry(case, qr, kr):
    if case == 0:
        start = max(qr - NA_KH // 2, 0)
        return start <= kr < start + NA_KH, kr - qr + NA_KH - 1
    if case == 1:
        return qr <= kr < qr + NA_KH, kr - qr + NA_KH - 1 - NA_KH // 2
    lo = NA_QROWS + min(qr - NA_KH // 2, 0)
    return lo <= kr < lo + NA_KH, kr - qr - 1


def _na_bias_kernel(rpb_ref, o_ref):
    h = pl.program_id(0)
    n_dr, n_dc = 2 * NA_KH - 1, 2 * NA_KW - 1
    qc = lax.broadcasted_iota(jnp.int32, (GRID_W, LANES), 0)
    lane = lax.broadcasted_iota(jnp.int32, (GRID_W, LANES), 1)
    kc = lane % GRID_W
    cstart = jnp.clip(qc - NA_KW // 2, 0, GRID_W - NA_KW)
    col_valid = (kc >= cstart) & (kc < cstart + NA_KW)
    dc_idx = jnp.clip(kc - qc + NA_KW - 1, 0, n_dc - 1)
    neg = jnp.full((GRID_W, LANES), NEG, F32)
    tiles = []
    for i in range(n_dr):
        acc = jnp.zeros((GRID_W, LANES), F32)
        for j in range(n_dc):
            acc = jnp.where(dc_idx == j, rpb_ref[h * (n_dr * n_dc) + i * n_dc + j], acc)
        tiles.append(jnp.where(col_valid, acc, neg))
    left = lane < GRID_W
    for case in range(3):
        for qr in range(NA_QROWS):
            for kp in range(NA_SLAB_ROWS // 2):
                v0, i0 = _na_case_geometry(case, qr, 2 * kp)
                v1, i1 = _na_case_geometry(case, qr, 2 * kp + 1)
                t0 = tiles[i0] if v0 else neg
                t1 = tiles[i1] if v1 else neg
                if v0 or v1:
                    blk = jnp.where(left, t0, t1)
                else:
                    blk = neg
                o_ref[case, 0, qr * GRID_W:(qr + 1) * GRID_W,
                      kp * LANES:(kp + 1) * LANES] = blk


def _na_bias(rpb):
    for case in range(3):
        for qr in range(NA_QROWS):
            for kr in range(NA_SLAB_ROWS):
                valid, idx = _na_case_geometry(case, qr, kr)
                assert not valid or 0 <= idx < 2 * NA_KH - 1
    return pl.pallas_call(
        _na_bias_kernel,
        grid=(NA_HEADS,),
        in_specs=[pl.BlockSpec(memory_space=pltpu.SMEM)],
        out_specs=pl.BlockSpec((3, 1, NA_BQ, NA_BK), lambda h: (0, h, 0, 0)),
        out_shape=jax.ShapeDtypeStruct((3, NA_HEADS, NA_BQ, NA_BK), F32),
        compiler_params=_params(1),
        name="na_bias",
    )(rpb.reshape(-1))


def _na_kernel(q_ref, k_ref, v_ref, b_ref, o_ref, *, n_rb, rows):
    rb = pl.program_id(1) % n_rb
    slab = jnp.clip(rb * NA_QROWS - NA_KH // 2, 0, rows - NA_SLAB_ROWS) * GRID_W
    slab = pl.multiple_of(slab, NA_KH // 2 * GRID_W)
    ks = k_ref[pl.ds(slab, NA_BK), :]
    vs = v_ref[pl.ds(slab, NA_BK), :]
    s = lax.dot_general(q_ref[...], ks, (((1,), (1,)), ((), ())),
                        preferred_element_type=F32)
    s = s * (NA_DIM ** -0.5) + b_ref[0, 0]
    m = jnp.max(s, axis=1, keepdims=True)
    p = jnp.exp(s - m)
    l = jnp.sum(p, axis=1, keepdims=True)
    o = jnp.dot(p.astype(BF16), vs, preferred_element_type=F32)
    o_ref[...] = (o / l).astype(BF16)


def _na_attention(qkv, bias, row_off, n_seq, seq_len):
    rows = seq_len // GRID_W
    n_rb = rows // NA_QROWS
    assert rows >= NA_SLAB_ROWS and row_off % seq_len == 0
    q_off = row_off // NA_BQ
    s_off = row_off // seq_len

    def case_of(i):
        rb = i % n_rb
        return jnp.where(rb == 0, 0, jnp.where(rb == n_rb - 1, 2, 1))

    return pl.pallas_call(
        functools.partial(_na_kernel, n_rb=n_rb, rows=rows),
        grid=(NA_HEADS, n_seq * n_rb),
        in_specs=[
            pl.BlockSpec((NA_BQ, NA_DIM), lambda h, i: (q_off + i, h)),
            pl.BlockSpec((seq_len, NA_DIM), lambda h, i: (s_off + i // n_rb, NA_HEADS + h)),
            pl.BlockSpec((seq_len, NA_DIM), lambda h, i: (s_off + i // n_rb, 2 * NA_HEADS + h)),
            pl.BlockSpec((1, 1, NA_BQ, NA_BK), lambda h, i: (case_of(i), h, 0, 0)),
        ],
        out_specs=pl.BlockSpec((NA_BQ, NA_DIM), lambda h, i: (i, h)),
        out_shape=jax.ShapeDtypeStruct((n_seq * seq_len, NA_WIDTH), BF16),
        compiler_params=_params(2),
        name="na_attn",
    )(qkv, qkv, qkv, bias)


def _rope_tables():
    half = QK_ROPE // 2
    inv = jnp.power(ROPE_THETA, -jnp.arange(0, QK_ROPE, 2, dtype=F32) / QK_ROPE)
    pos = jnp.concatenate([jnp.arange(L_S, dtype=F32),
                           jnp.tile(jnp.arange(L_P, dtype=F32), B_P)])
    ang = pos[:, None] * inv[None, :]
    cos, sin = jnp.cos(ang), jnp.sin(ang)
    z = jnp.zeros((T, LANES - QK_ROPE), F32)
    zh = jnp.zeros((T, half), F32)
    c_tab = jnp.concatenate([cos, cos, z], axis=1)
    sp_tab = jnp.concatenate([zh, sin, z], axis=1)
    sm_tab = jnp.concatenate([-sin, zh, z], axis=1)
    return c_tab, sp_tab, sm_tab


def _apply_rope(r, c_ref, sp_ref, sm_ref):
    half = QK_ROPE // 2
    return (r * c_ref[...] + pltpu.roll(r, half, 1) * sp_ref[...]
            + pltpu.roll(r, LANES - half, 1) * sm_ref[...])


def _krope_kernel(x_ref, c_ref, sp_ref, sm_ref, o_ref):
    r = _apply_rope(x_ref[...], c_ref, sp_ref, sm_ref)
    o_ref[...] = r.T.astype(BF16)


def _krope(lat, tabs):
    bm = 512
    tab_spec = pl.BlockSpec((bm, LANES), lambda i: (i, 0))
    return pl.pallas_call(
        _krope_kernel,
        grid=(T // bm,),
        in_specs=[pl.BlockSpec((bm, LANES), lambda i: (i, LAT_ROPE // LANES)),
                  tab_spec, tab_spec, tab_spec],
        out_specs=pl.BlockSpec((LANES, bm), lambda i: (0, i)),
        out_shape=jax.ShapeDtypeStruct((LANES, T), BF16),
        compiler_params=_params(1),
        name="mla_krope",
    )(lat, *tabs)


def _uq_kernel(x_ref, g_ref, w_ref, c_ref, sp_ref, sm_ref, o_ref, xn_ref):
    @pl.when(pl.program_id(1) == 0)
    def _():
        x = x_ref[...]
        ms = jnp.mean(x * x, axis=-1, keepdims=True)
        xn_ref[...] = (x * lax.rsqrt(ms + EPS) * g_ref[...]).astype(BF16)

    acc = jnp.dot(xn_ref[...], w_ref[...], preferred_element_type=F32)
    scale = (QK_NOPE + QK_ROPE) ** -0.5
    o_ref[:, :LANES] = (acc[:, :LANES] * scale).astype(BF16)
    rope = _apply_rope(acc[:, LANES:], c_ref, sp_ref, sm_ref)
    o_ref[:, LANES:] = (rope * scale).astype(BF16)


def _uq(lat, g_q, w_uq_p, tabs):
    bm = 512
    hw = 2 * LANES
    tab_spec = pl.BlockSpec((bm, LANES), lambda i, j: (i, 0))
    return pl.pallas_call(
        _uq_kernel,
        grid=(T // bm, MLA_HEADS),
        in_specs=[pl.BlockSpec((bm, Q_LORA), lambda i, j: (i, LAT_Q // Q_LORA)),
                  pl.BlockSpec((1, Q_LORA), lambda i, j: (0, 0)),
                  pl.BlockSpec((Q_LORA, hw), lambda i, j: (0, j)),
                  tab_spec, tab_spec, tab_spec],
        out_specs=pl.BlockSpec((bm, hw), lambda i, j: (i, j)),
        out_shape=jax.ShapeDtypeStruct((T, MLA_HEADS * hw), BF16),
        scratch_shapes=[pltpu.VMEM((bm, Q_LORA), BF16)],
        compiler_params=_params(2),
        name="mla_uq",
    )(lat, g_q.reshape(1, Q_LORA), w_uq_p, *tabs)


def _ukv_kernel(x_ref, g_ref, wk_ref, wv_ref, kpe_ref, kt_ref, v_ref, xn_ref):
    @pl.when(pl.program_id(1) == 0)
    def _():
        x = x_ref[...]
        ms = jnp.mean(x * x, axis=-1, keepdims=True)
        xn_ref[...] = (x * lax.rsqrt(ms + EPS) * g_ref[...]).astype(BF16)

    xn = xn_ref[...]
    kt = lax.dot_general(wk_ref[...], xn, (((1,), (1,)), ((), ())),
                         preferred_element_type=F32)
    kt_ref[:QK_NOPE, :] = kt.astype(BF16)
    kt_ref[QK_NOPE:, :] = kpe_ref[...]
    v_ref[...] = jnp.dot(xn, wv_ref[...], preferred_element_type=F32).astype(BF16)


def _ukv(lat, g_kv, w_kt, w_v, kpe_t):
    bm = 512
    hw = 2 * LANES
    return pl.pallas_call(
        _ukv_kernel,
        grid=(T // bm, MLA_HEADS),
        in_specs=[pl.BlockSpec((bm, KV_LORA), lambda i, j: (i, LAT_KV // KV_LORA)),
                  pl.BlockSpec((1, KV_LORA), lambda i, j: (0, 0)),
                  pl.BlockSpec((QK_NOPE, KV_LORA), lambda i, j: (j, 0)),
                  pl.BlockSpec((KV_LORA, V_HEAD), lambda i, j: (0, j)),
                  pl.BlockSpec((LANES, bm), lambda i, j: (0, i))],
        out_specs=[pl.BlockSpec((hw, bm), lambda i, j: (j, i)),
                   pl.BlockSpec((bm, V_HEAD), lambda i, j: (i, j))],
        out_shape=[jax.ShapeDtypeStruct((MLA_HEADS * hw, T), BF16),
                   jax.ShapeDtypeStruct((T, MLA_HEADS * V_HEAD), BF16)],
        scratch_shapes=[pltpu.VMEM((bm, KV_LORA), BF16)],
        compiler_params=_params(2),
        name="mla_ukv",
    )(lat, g_kv.reshape(1, KV_LORA), w_kt, w_v, kpe_t)


def _flash_kernel(q_ref, kt_ref, v_ref, o_ref, m_ref, l_ref, acc_ref):
    ki = pl.program_id(3)

    @pl.when(ki == 0)
    def _():
        m_ref[...] = jnp.full(m_ref.shape, -jnp.inf, F32)
        l_ref[...] = jnp.zeros(l_ref.shape, F32)
        acc_ref[...] = jnp.zeros(acc_ref.shape, F32)

    s = jnp.dot(q_ref[...], kt_ref[...], preferred_element_type=F32)
    m_prev = m_ref[...]
    m_new = jnp.maximum(m_prev, jnp.max(s, axis=1, keepdims=True))
    alpha = jnp.exp(m_prev - m_new)
    p = jnp.exp(s - m_new)
    l_ref[...] = alpha * l_ref[...] + jnp.sum(p, axis=1, keepdims=True)
    acc_ref[...] = alpha * acc_ref[...] + jnp.dot(p.astype(BF16), v_ref[...],
                                                  preferred_element_type=F32)
    m_ref[...] = m_new

    @pl.when(ki == pl.num_programs(3) - 1)
    def _():
        o_ref[...] = (acc_ref[...] / l_ref[...]).astype(BF16)


def _mla_attention(q, kt, v, row_off, n_seq, seq_len):
    bq, bkv = 1024, 1024
    hw = 2 * LANES
    nq, nk = seq_len // bq, seq_len // bkv
    q_off, k_off = row_off // bq, row_off // bkv
    return pl.pallas_call(
        _flash_kernel,
        grid=(n_seq, MLA_HEADS, nq, nk),
        in_specs=[
            pl.BlockSpec((bq, hw), lambda b, h, qi, ki: (q_off + b * nq + qi, h)),
            pl.BlockSpec((hw, bkv), lambda b, h, qi, ki: (h, k_off + b * nk + ki)),
            pl.BlockSpec((bkv, V_HEAD), lambda b, h, qi, ki: (k_off + b * nk + ki, h)),
        ],
        out_specs=pl.BlockSpec((bq, V_HEAD), lambda b, h, qi, ki: (b * nq + qi, h)),
        out_shape=jax.ShapeDtypeStruct((n_seq * seq_len, MLA_HEADS * V_HEAD), BF16),
        scratch_shapes=[pltpu.VMEM((bq, 1), F32), pltpu.VMEM((bq, 1), F32),
                        pltpu.VMEM((bq, V_HEAD), F32)],
        compiler_params=_params(4),
        name="mla_flash",
    )(q, kt, v)


def _merge_kernel(ona_ref, omla_ref, wna_ref, wmla_ref, gna_ref, gmla_ref, o_ref):
    pn = jnp.dot(ona_ref[...], wna_ref[...], preferred_element_type=F32)
    pm = jnp.dot(omla_ref[...], wmla_ref[...], preferred_element_type=F32)
    o_ref[...] = (jax.nn.sigmoid(gna_ref[...].astype(F32)) * pn
                  + jax.nn.sigmoid(gmla_ref[...].astype(F32)) * pm).astype(BF16)


def _merge(o_na, o_mla, w_na, w_mla, gates):
    bm, bn = 512, 1024
    return pl.pallas_call(
        _merge_kernel,
        grid=(T // bm, D // bn),
        in_specs=[pl.BlockSpec((bm, NA_WIDTH), lambda i, j: (i, 0)),
                  pl.BlockSpec((bm, MLA_HEADS * V_HEAD), lambda i, j: (i, 0)),
                  pl.BlockSpec((NA_WIDTH, bn), lambda i, j: (0, j)),
                  pl.BlockSpec((MLA_HEADS * V_HEAD, bn), lambda i, j: (0, j)),
                  pl.BlockSpec((bm, bn), lambda i, j: (i, j)),
                  pl.BlockSpec((bm, bn), lambda i, j: (i, D // bn + j))],
        out_specs=pl.BlockSpec((bm, bn), lambda i, j: (i, j)),
        out_shape=jax.ShapeDtypeStruct((T, D), BF16),
        compiler_params=_params(2),
        name="merge_proj",
    )(o_na, o_mla, w_na, w_mla, gates, gates)


def _residual_epilogue(o_ref, x_ref, mod_ref, g_ref, gate_idx):
    y = o_ref[...]
    ms = jnp.mean(y * y, axis=-1, keepdims=True)
    n = y * lax.rsqrt(ms + EPS) * g_ref[...]
    gate = mod_ref[0, gate_idx:gate_idx + 1, :]
    o_ref[...] = x_ref[...] + gate * n


def _out_proj_kernel(a_ref, w_ref, x_ref, mod_ref, g_ref, o_ref, *, gate_idx):
    k = pl.program_id(1)
    part = jnp.dot(a_ref[...], w_ref[...], preferred_element_type=F32)

    @pl.when(k == 0)
    def _():
        o_ref[...] = part

    @pl.when(k > 0)
    def _():
        o_ref[...] += part

    @pl.when(k == pl.num_programs(1) - 1)
    def _():
        _residual_epilogue(o_ref, x_ref, mod_ref, g_ref, gate_idx)


def _out_proj(a, w, x, mod3, g, gate_idx):
    bm, bk = 512, 512
    kdim = a.shape[1]
    return pl.pallas_call(
        functools.partial(_out_proj_kernel, gate_idx=gate_idx),
        grid=(T // bm, kdim // bk),
        in_specs=[pl.BlockSpec((bm, bk), lambda i, k: (i, k)),
                  pl.BlockSpec((bk, D), lambda i, k: (k, 0)),
                  pl.BlockSpec((bm, D), lambda i, k: (i, 0), pipeline_mode=pl.Buffered(1)),
                  pl.BlockSpec((1, N_MOD, D), lambda i, k: (_seq_of_block(i, bm), 0, 0)),
                  pl.BlockSpec((1, D), lambda i, k: (0, 0))],
        out_specs=pl.BlockSpec((bm, D), lambda i, k: (i, 0)),
        out_shape=jax.ShapeDtypeStruct((T, D), F32),
        compiler_params=_params(2),
        name="out_proj",
    )(a, w, x, mod3, g.reshape(1, D))


def _gelu_tanh(x):
    c = float(np.sqrt(2.0 / np.pi))
    return 0.5 * x * (1.0 + jnp.tanh(c * (x + 0.044715 * (x * x * x))))


def _ffn_down_kernel(a_ref, ap_ref, an_ref, u_ref, cw_ref, cb_ref, w_ref,
                     x_ref, mod_ref, g_ref, o_ref, *, bm, gate_idx):
    i = pl.program_id(0)
    k = pl.program_id(1)
    t0 = i * bm
    t1 = t0 + bm
    seq_start = (t0 == 0) | ((t0 >= T_S) & ((t0 - T_S) % L_P == 0))
    seq_end = (t1 == T_S) | ((t1 > T_S) & ((t1 - T_S) % L_P == 0))
    a = a_ref[...].astype(F32)
    prev_row = ap_ref[BF16_SUBLANES - 1:BF16_SUBLANES, :].astype(F32)
    next_row = an_ref[0:1, :].astype(F32)
    prev_row = jnp.where(seq_start, 0.0, prev_row)
    next_row = jnp.where(seq_end, 0.0, next_row)
    row = lax.broadcasted_iota(jnp.int32, a.shape, 0)
    a_dn = jnp.where(row == 0, prev_row, pltpu.roll(a, 1, 0))
    a_up = jnp.where(row == bm - 1, next_row, pltpu.roll(a, bm - 1, 0))
    conv = a_dn * cw_ref[0:1, :] + a * cw_ref[1:2, :] + a_up * cw_ref[2:3, :] + cb_ref[...]
    gated = (_gelu_tanh(conv) * u_ref[...].astype(F32)).astype(BF16)
    part = jnp.dot(gated, w_ref[...], preferred_element_type=F32)

    @pl.when(k == 0)
    def _():
        o_ref[...] = part

    @pl.when(k > 0)
    def _():
        o_ref[...] += part

    @pl.when(k == pl.num_programs(1) - 1)
    def _():
        _residual_epilogue(o_ref, x_ref, mod_ref, g_ref, gate_idx)


def _ffn_down(au, conv_w, conv_b, w_down, x, mod3, g, gate_idx):
    bm, bk = 512, 512
    nk = D_FF_PAD // bk
    halo = BF16_SUBLANES
    last_halo = T // halo - 1
    return pl.pallas_call(
        functools.partial(_ffn_down_kernel, bm=bm, gate_idx=gate_idx),
        grid=(T // bm, nk),
        in_specs=[
            pl.BlockSpec((bm, bk), lambda i, k: (i, k)),
            pl.BlockSpec((halo, bk), lambda i, k: (jnp.maximum(i * (bm // halo) - 1, 0), k)),
            pl.BlockSpec((halo, bk), lambda i, k: (jnp.minimum((i + 1) * (bm // halo), last_halo), k)),
            pl.BlockSpec((bm, bk), lambda i, k: (i, nk + k)),
            pl.BlockSpec((3, bk), lambda i, k: (0, k)),
            pl.BlockSpec((1, bk), lambda i, k: (0, k)),
            pl.BlockSpec((bk, D), lambda i, k: (k, 0)),
            pl.BlockSpec((bm, D), lambda i, k: (i, 0), pipeline_mode=pl.Buffered(1)),
            pl.BlockSpec((1, N_MOD, D), lambda i, k: (_seq_of_block(i, bm), 0, 0)),
            pl.BlockSpec((1, D), lambda i, k: (0, 0)),
        ],
        out_specs=pl.BlockSpec((bm, D), lambda i, k: (i, 0)),
        out_shape=jax.ShapeDtypeStruct((T, D), F32),
        compiler_params=_params(2),
        name="ffn_down",
    )(au, au, au, au, conv_w, conv_b, w_down, x, mod3, g.reshape(1, D))


def _prep_weights(w_in, w_uq, w_ukv, w_ffn_in, conv_w, conv_b, w_ffn_down):
    c_qkv = 3 * NA_WIDTH
    c_q = c_qkv + Q_LORA
    c_kv = c_q + KV_LORA
    c_r = c_kv + QK_ROPE
    w_qkv = w_in[:, :c_qkv].astype(BF16)
    w_lat = jnp.zeros((D, LAT_W), F32)
    w_lat = w_lat.at[:, LAT_Q:LAT_Q + Q_LORA].set(w_in[:, c_qkv:c_q])
    w_lat = w_lat.at[:, LAT_KV:LAT_KV + KV_LORA].set(w_in[:, c_q:c_kv])
    w_lat = w_lat.at[:, LAT_ROPE:LAT_ROPE + QK_ROPE].set(w_in[:, c_kv:c_r])
    w_lat = w_lat.astype(BF16)
    w_gate = w_in[:, c_r:].astype(BF16)
    uq = w_uq.reshape(Q_LORA, MLA_HEADS, QK_NOPE + QK_ROPE)
    uq = jnp.concatenate([uq, jnp.zeros((Q_LORA, MLA_HEADS, LANES - QK_ROPE), F32)], axis=2)
    w_uq_p = uq.reshape(Q_LORA, MLA_HEADS * 2 * LANES).astype(BF16)
    ukv = w_ukv.reshape(KV_LORA, MLA_HEADS, QK_NOPE + V_HEAD)
    w_kt = ukv[:, :, :QK_NOPE].reshape(KV_LORA, MLA_HEADS * QK_NOPE).T.astype(BF16)
    w_v = ukv[:, :, QK_NOPE:].reshape(KV_LORA, MLA_HEADS * V_HEAD).astype(BF16)
    pad = D_FF_PAD - D_FF
    w_a = jnp.pad(w_ffn_in[:, :D_FF], ((0, 0), (0, pad)))
    w_u = jnp.pad(w_ffn_in[:, D_FF:], ((0, 0), (0, pad)))
    w_au = jnp.concatenate([w_a, w_u], axis=1).astype(BF16)
    cw = jnp.pad(conv_w, ((0, 0), (0, pad)))
    cb = jnp.pad(conv_b, ((0, pad),)).reshape(1, D_FF_PAD)
    w_down = jnp.pad(w_ffn_down, ((0, pad), (0, 0))).astype(BF16)
    return w_qkv, w_lat, w_gate, w_uq_p, w_kt, w_v, w_au, cw, cb, w_down


def kernel(x_prompt, x_sample, c_prompt, c_sample, w_ada, b_ada, g_pre_mix, g_post_mix, w_in, rpb,
           g_q, w_uq, g_kv, w_ukv, w_na_proj, w_mla_proj, w_out, g_pre_ffn, g_post_ffn,
           w_ffn_in, conv_w, conv_b, w_ffn_down):
    assert w_ada.shape[0] == 1
    x = jnp.concatenate([x_sample.reshape(T_S, D), x_prompt.reshape(T_P, D)], axis=0)
    c8 = jnp.concatenate([c_sample, c_prompt, jnp.zeros((8 - N_SEQ, D), F32)], axis=0)
    (w_qkv, w_lat, w_gate, w_uq_p, w_kt, w_v, w_au, cw, cb, w_down) = _prep_weights(
        w_in[0], w_uq[0], w_ukv[0], w_ffn_in[0], conv_w[0], conv_b[0], w_ffn_down[0])

    mod3 = _ada(c8, w_ada[0], b_ada[0]).reshape(8, N_MOD, D)

    h1 = _norm_mod(x, g_pre_mix[0], mod3, 0, 1)
    qkv = _matmul(h1, w_qkv, BF16, 1024, 1024, "in_proj_qkv")
    lat = _matmul(h1, w_lat, F32, 1024, 1024, "in_proj_lat")
    gates = _matmul(h1, w_gate, BF16, 1024, 1024, "in_proj_gate")

    bias = _na_bias(rpb[0])
    o_na = jnp.concatenate([_na_attention(qkv, bias, 0, 1, L_S),
                            _na_attention(qkv, bias, T_S, B_P, L_P)], axis=0)

    tabs = _rope_tables()
    kpe_t = _krope(lat, tabs)
    q = _uq(lat, g_q[0], w_uq_p, tabs)
    kt, v = _ukv(lat, g_kv[0], w_kt, w_v, kpe_t)
    o_mla = jnp.concatenate([_mla_attention(q, kt, v, 0, 1, L_S),
                             _mla_attention(q, kt, v, T_S, B_P, L_P)], axis=0)

    merged = _merge(o_na, o_mla, w_na_proj[0].astype(BF16), w_mla_proj[0].astype(BF16), gates)
    x1 = _out_proj(merged, w_out[0].astype(BF16), x, mod3, g_post_mix[0], 2)

    h2 = _norm_mod(x1, g_pre_ffn[0], mod3, 3, 4)
    au = _matmul(h2, w_au, BF16, 1024, 1024, "ffn_in")
    out = _out_ffn(au, cw, cb, w_down, x1, mod3, g_post_ffn[0])

    y_sample = out[:T_S].reshape(1, L_S, D)
    y_prompt = out[T_S:].reshape(B_P, L_P, D)
    return (y_prompt, y_sample)


def _out_ffn(au, cw, cb, w_down, x1, mod3, g):
    return _ffn_down(au, cw, cb, w_down, x1, mod3, g, 5)
```

```python
import functools

import numpy as np
import jax
import jax.numpy as jnp
from jax import lax
from jax.experimental import pallas as pl
from jax.experimental.pallas import tpu as pltpu

F32 = jnp.float32
BF16 = jnp.bfloat16

D = 4096
B_P, L_P = 4, 2048
L_S = 16384
T_S = L_S
T_P = B_P * L_P
T = T_S + T_P
N_SEQ = 1 + B_P
N_MOD = 6
GRID_W = 64
NA_HEADS, NA_DIM = 16, 128
NA_WIDTH = NA_HEADS * NA_DIM
NA_KH, NA_KW = 8, 16
MLA_HEADS = 16
Q_LORA, KV_LORA = 896, 512
QK_NOPE, QK_ROPE, V_HEAD = 128, 64, 128
ROPE_THETA = 10000.0
D_FF = 11008
D_FF_PAD = 11264
EPS = 1e-6
NEG = -1e30

LAT_W = 2048
LAT_Q, LAT_KV, LAT_ROPE = 0, 1024, 1536

V7X_VMEM_BYTES = 64 * 1024 * 1024
VMEM_LIMIT = V7X_VMEM_BYTES - 8 * 1024 * 1024
LANES = 128
BF16_SUBLANES = 16

NA_QROWS = 8
NA_SLAB_ROWS = 16
NA_BQ = NA_QROWS * GRID_W
NA_BK = NA_SLAB_ROWS * GRID_W


def _params(n_axes):
    return pltpu.CompilerParams(dimension_semantics=("arbitrary",) * n_axes,
                                vmem_limit_bytes=VMEM_LIMIT)


def _seq_of_block(i, bm):
    t0 = i * bm
    return jnp.where(t0 < T_S, 0, 1 + jnp.maximum(t0 - T_S, 0) // L_P)


def _ada_kernel(c_ref, w_ref, b_ref, o_ref):
    c = c_ref[...]
    s = c * jax.nn.sigmoid(c)
    o_ref[...] = jnp.dot(s.astype(BF16), w_ref[...].astype(BF16),
                         preferred_element_type=F32) + b_ref[...]


def _ada(c8, w_ada, b_ada):
    bn = 512
    n = N_MOD * D
    return pl.pallas_call(
        _ada_kernel,
        grid=(n // bn,),
        in_specs=[pl.BlockSpec((8, D), lambda j: (0, 0)),
                  pl.BlockSpec((D, bn), lambda j: (0, j)),
                  pl.BlockSpec((1, bn), lambda j: (0, j))],
        out_specs=pl.BlockSpec((8, bn), lambda j: (0, j)),
        out_shape=jax.ShapeDtypeStruct((8, n), F32),
        compiler_params=_params(1),
        name="ada_mod",
    )(c8, w_ada, b_ada.reshape(1, n))


def _norm_mod_kernel(x_ref, g_ref, mod_ref, o_ref, *, shift_idx, scale_idx):
    x = x_ref[...]
    ms = jnp.mean(x * x, axis=-1, keepdims=True)
    y = x * lax.rsqrt(ms + EPS) * g_ref[...]
    shift = mod_ref[0, shift_idx:shift_idx + 1, :]
    scale = mod_ref[0, scale_idx:scale_idx + 1, :]
    o_ref[...] = (y * (1.0 + scale) + shift).astype(BF16)


def _norm_mod(x, g, mod3, shift_idx, scale_idx):
    bm = 256
    return pl.pallas_call(
        functools.partial(_norm_mod_kernel, shift_idx=shift_idx, scale_idx=scale_idx),
        grid=(T // bm,),
        in_specs=[pl.BlockSpec((bm, D), lambda i: (i, 0)),
                  pl.BlockSpec((1, D), lambda i: (0, 0)),
                  pl.BlockSpec((1, N_MOD, D), lambda i: (_seq_of_block(i, bm), 0, 0))],
        out_specs=pl.BlockSpec((bm, D), lambda i: (i, 0)),
        out_shape=jax.ShapeDtypeStruct((T, D), BF16),
        compiler_params=_params(1),
        name="norm_mod",
    )(x, g.reshape(1, D), mod3)


def _mm_kernel(a_ref, w_ref, o_ref):
    o_ref[...] = jnp.dot(a_ref[...], w_ref[...],
                         preferred_element_type=F32).astype(o_ref.dtype)


def _matmul(a, w, out_dtype, bm, bn, name):
    m, k = a.shape
    _, n = w.shape
    return pl.pallas_call(
        _mm_kernel,
        grid=(m // bm, n // bn),
        in_specs=[pl.BlockSpec((bm, k), lambda i, j: (i, 0)),
                  pl.BlockSpec((k, bn), lambda i, j: (0, j))],
        out_specs=pl.BlockSpec((bm, bn), lambda i, j: (i, j)),
        out_shape=jax.ShapeDtypeStruct((m, n), out_dtype),
        compiler_params=_params(2),
        name=name,
    )(a, w)


def _na_case_geometry(case, qr, kr):
    if case == 0:
        start = max(qr - NA_KH // 2, 0)
        return start <= kr < start + NA_KH, kr - qr + NA_KH - 1
    if case == 1:
        return qr <= kr < qr + NA_KH, kr - qr + NA_KH - 1 - NA_KH // 2
    lo = NA_QROWS + min(qr - NA_KH // 2, 0)
    return lo <= kr < lo + NA_KH, kr - qr - 1


def _na_bias_kernel(rpb_ref, o_ref):
    h = pl.program_id(0)
    n_dr, n_dc = 2 * NA_KH - 1, 2 * NA_KW - 1
    qc = lax.broadcasted_iota(jnp.int32, (GRID_W, LANES), 0)
    lane = lax.broadcasted_iota(jnp.int32, (GRID_W, LANES), 1)
    kc = lane % GRID_W
    cstart = jnp.clip(qc - NA_KW // 2, 0, GRID_W - NA_KW)
    col_valid = (kc >= cstart) & (kc < cstart + NA_KW)
    dc_idx = jnp.clip(kc - qc + NA_KW - 1, 0, n_dc - 1)
    neg = jnp.full((GRID_W, LANES), NEG, F32)
    tiles = []
    for i in range(n_dr):
        acc = jnp.zeros((GRID_W, LANES), F32)
        for j in range(n_dc):
            acc = jnp.where(dc_idx == j, rpb_ref[h * (n_dr * n_dc) + i * n_dc + j], acc)
        tiles.append(jnp.where(col_valid, acc, neg))
    left = lane < GRID_W
    for case in range(3):
        for qr in range(NA_QROWS):
            for kp in range(NA_SLAB_ROWS // 2):
                v0, i0 = _na_case_geometry(case, qr, 2 * kp)
                v1, i1 = _na_case_geometry(case, qr, 2 * kp + 1)
                t0 = tiles[i0] if v0 else neg
                t1 = tiles[i1] if v1 else neg
                if v0 or v1:
                    blk = jnp.where(left, t0, t1)
                else:
                    blk = neg
                o_ref[case, 0, qr * GRID_W:(qr + 1) * GRID_W,
                      kp * LANES:(kp + 1) * LANES] = blk


def _na_bias(rpb):
    for case in range(3):
        for qr in range(NA_QROWS):
            for kr in range(NA_SLAB_ROWS):
                valid, idx = _na_case_geometry(case, qr, kr)
                assert not valid or 0 <= idx < 2 * NA_KH - 1
    return pl.pallas_call(
        _na_bias_kernel,
        grid=(NA_HEADS,),
        in_specs=[pl.BlockSpec(memory_space=pltpu.SMEM)],
        out_specs=pl.BlockSpec((3, 1, NA_BQ, NA_BK), lambda h: (0, h, 0, 0)),
        out_shape=jax.ShapeDtypeStruct((3, NA_HEADS, NA_BQ, NA_BK), F32),
        compiler_params=_params(1),
        name="na_bias",
    )(rpb.reshape(-1))


def _na_kernel(q_ref, k_ref, v_ref, b_ref, o_ref, *, n_rb, rows):
    rb = pl.program_id(1) % n_rb
    slab = jnp.clip(rb * NA_QROWS - NA_KH // 2, 0, rows - NA_SLAB_ROWS) * GRID_W
    slab = pl.multiple_of(slab, NA_KH // 2 * GRID_W)
    ks = k_ref[pl.ds(slab, NA_BK), :]
    vs = v_ref[pl.ds(slab, NA_BK), :]
    s = lax.dot_general(q_ref[...], ks, (((1,), (1,)), ((), ())),
                        preferred_element_type=F32)
    s = s * (NA_DIM ** -0.5) + b_ref[0, 0]
    m = jnp.max(s, axis=1, keepdims=True)
    p = jnp.exp(s - m)
    l = jnp.sum(p, axis=1, keepdims=True)
    o = jnp.dot(p.astype(BF16), vs, preferred_element_type=F32)
    o_ref[...] = (o / l).astype(BF16)


def _na_attention(qkv, bias, row_off, n_seq, seq_len):
    rows = seq_len // GRID_W
    n_rb = rows // NA_QROWS
    assert rows >= NA_SLAB_ROWS and row_off % seq_len == 0
    q_off = row_off // NA_BQ
    s_off = row_off // seq_len

    def case_of(i):
        rb = i % n_rb
        return jnp.where(rb == 0, 0, jnp.where(rb == n_rb - 1, 2, 1))

    return pl.pallas_call(
        functools.partial(_na_kernel, n_rb=n_rb, rows=rows),
        grid=(NA_HEADS, n_seq * n_rb),
        in_specs=[
            pl.BlockSpec((NA_BQ, NA_DIM), lambda h, i: (q_off + i, h)),
            pl.BlockSpec((seq_len, NA_DIM), lambda h, i: (s_off + i // n_rb, NA_HEADS + h)),
            pl.BlockSpec((seq_len, NA_DIM), lambda h, i: (s_off + i // n_rb, 2 * NA_HEADS + h)),
            pl.BlockSpec((1, 1, NA_BQ, NA_BK), lambda h, i: (case_of(i), h, 0, 0)),
        ],
        out_specs=pl.BlockSpec((NA_BQ, NA_DIM), lambda h, i: (i, h)),
        out_shape=jax.ShapeDtypeStruct((n_seq * seq_len, NA_WIDTH), BF16),
        compiler_params=_params(2),
        name="na_attn",
    )(qkv, qkv, qkv, bias)


def _rope_tables():
    half = QK_ROPE // 2
    inv = jnp.power(ROPE_THETA, -jnp.arange(0, QK_ROPE, 2, dtype=F32) / QK_ROPE)
    pos = jnp.concatenate([jnp.arange(L_S, dtype=F32),
                           jnp.tile(jnp.arange(L_P, dtype=F32), B_P)])
    ang = pos[:, None] * inv[None, :]
    cos, sin = jnp.cos(ang), jnp.sin(ang)
    z = jnp.zeros((T, LANES - QK_ROPE), F32)
    zh = jnp.zeros((T, half), F32)
    c_tab = jnp.concatenate([cos, cos, z], axis=1)
    sp_tab = jnp.concatenate([zh, sin, z], axis=1)
    sm_tab = jnp.concatenate([-sin, zh, z], axis=1)
    return (c_tab, sp_tab, sm_tab), (cos.T, sin.T)


def _apply_rope(r, c_ref, sp_ref, sm_ref):
    half = QK_ROPE // 2
    return (r * c_ref[...] + pltpu.roll(r, half, 1) * sp_ref[...]
            + pltpu.roll(r, LANES - half, 1) * sm_ref[...])


MLA_HW = 2 * LANES
MLA_BKV = 1024
LOG2E = float(np.log2(np.e))


def _rms_norm_bf16(x, g):
    ms = jnp.mean(x * x, axis=-1, keepdims=True)
    return (x * lax.rsqrt(ms + EPS) * g).astype(BF16)


def _uq_kernel(x_ref, g_ref, w_ref, cos_ref, sin_ref, o_ref):
    xn = _rms_norm_bf16(x_ref[...], g_ref[...])
    acc = lax.dot_general(w_ref[...], xn, (((1,), (1,)), ((), ())),
                          preferred_element_type=F32)
    scale = (QK_NOPE + QK_ROPE) ** -0.5 * LOG2E
    half = QK_ROPE // 2
    cos = cos_ref[...] * scale
    sin = sin_ref[...] * scale
    for h in range(MLA_HEADS):
        r0 = h * MLA_HW
        o_ref[r0:r0 + QK_NOPE, :] = (acc[r0:r0 + QK_NOPE, :] * scale).astype(BF16)
        x1 = acc[r0 + QK_NOPE:r0 + QK_NOPE + half, :]
        x2 = acc[r0 + QK_NOPE + half:r0 + QK_NOPE + QK_ROPE, :]
        o_ref[r0 + QK_NOPE:r0 + QK_NOPE + half, :] = (x1 * cos - x2 * sin).astype(BF16)
        o_ref[r0 + QK_NOPE + half:r0 + QK_NOPE + QK_ROPE, :] = (x1 * sin + x2 * cos).astype(BF16)
        o_ref[r0 + QK_NOPE + QK_ROPE:r0 + MLA_HW, :] = jnp.zeros(
            (MLA_HW - QK_NOPE - QK_ROPE, o_ref.shape[1]), BF16)


def _uq(lat, g_q, w_uq_t, tabs_t):
    bm = 512
    half = QK_ROPE // 2
    tab_spec = pl.BlockSpec((half, bm), lambda i: (0, i))
    return pl.pallas_call(
        _uq_kernel,
        grid=(T // bm,),
        in_specs=[pl.BlockSpec((bm, Q_LORA), lambda i: (i, LAT_Q // Q_LORA)),
                  pl.BlockSpec((1, Q_LORA), lambda i: (0, 0)),
                  pl.BlockSpec((MLA_HEADS * MLA_HW, Q_LORA), lambda i: (0, 0)),
                  tab_spec, tab_spec],
        out_specs=pl.BlockSpec((MLA_HEADS * MLA_HW, bm), lambda i: (0, i)),
        out_shape=jax.ShapeDtypeStruct((MLA_HEADS * MLA_HW, T), BF16),
        compiler_params=_params(1),
        name="mla_uq",
    )(lat, g_q.reshape(1, Q_LORA), w_uq_t, *tabs_t)


def _ukv_kernel(x_ref, r_ref, g_ref, wk_ref, wv_ref, c_ref, sp_ref, sm_ref, k_ref, vt_ref):
    xn = _rms_norm_bf16(x_ref[...], g_ref[...])
    kn = jnp.dot(xn, wk_ref[...], preferred_element_type=F32)
    kpe = _apply_rope(r_ref[...], c_ref, sp_ref, sm_ref).astype(BF16)
    vt = lax.dot_general(wv_ref[...], xn, (((1,), (1,)), ((), ())),
                         preferred_element_type=F32)
    for h in range(MLA_HEADS):
        k_ref[:, h * MLA_HW:h * MLA_HW + QK_NOPE] = kn[:, h * QK_NOPE:(h + 1) * QK_NOPE].astype(BF16)
        k_ref[:, h * MLA_HW + QK_NOPE:(h + 1) * MLA_HW] = kpe
        vt_ref[h, 0] = vt[h * V_HEAD:(h + 1) * V_HEAD, :].astype(BF16)


def _ukv(lat, g_kv, w_k, w_vt, tabs):
    bm = MLA_BKV
    tab_spec = pl.BlockSpec((bm, LANES), lambda i: (i, 0))
    return pl.pallas_call(
        _ukv_kernel,
        grid=(T // bm,),
        in_specs=[pl.BlockSpec((bm, KV_LORA), lambda i: (i, LAT_KV // KV_LORA)),
                  pl.BlockSpec((bm, LANES), lambda i: (i, LAT_ROPE // LANES)),
                  pl.BlockSpec((1, KV_LORA), lambda i: (0, 0)),
                  pl.BlockSpec((KV_LORA, MLA_HEADS * QK_NOPE), lambda i: (0, 0)),
                  pl.BlockSpec((MLA_HEADS * V_HEAD, KV_LORA), lambda i: (0, 0)),
                  tab_spec, tab_spec, tab_spec],
        out_specs=[pl.BlockSpec((bm, MLA_HEADS * MLA_HW), lambda i: (i, 0)),
                   pl.BlockSpec((MLA_HEADS, 1, V_HEAD, bm), lambda i: (0, i, 0, 0))],
        out_shape=[jax.ShapeDtypeStruct((T, MLA_HEADS * MLA_HW), BF16),
                   jax.ShapeDtypeStruct((MLA_HEADS, T // bm, V_HEAD, bm), BF16)],
        compiler_params=_params(1),
        name="mla_ukv",
    )(lat, lat, g_kv.reshape(1, KV_LORA), w_k, w_vt, *tabs)


def _flash_kernel(qt_ref, k_ref, vt_ref, o_ref, acc_ref, *, nk):
    bq = qt_ref.shape[1]
    acc_ref[...] = jnp.zeros(acc_ref.shape, F32)

    def step(j, carry):
        m_prev, l_prev = carry
        start = pl.multiple_of(j * MLA_BKV, MLA_BKV)
        s = jnp.dot(k_ref[pl.ds(start, MLA_BKV), :], qt_ref[...],
                    preferred_element_type=F32)
        m_new = jnp.maximum(m_prev, jnp.max(s, axis=0, keepdims=True))
        alpha = jnp.exp2(m_prev - m_new)
        p = jnp.exp2(s - m_new)
        l_new = alpha * l_prev + jnp.sum(p, axis=0, keepdims=True)
        acc_ref[...] = alpha * acc_ref[...] + jnp.dot(
            vt_ref[0, j], p.astype(BF16), preferred_element_type=F32)
        return m_new, l_new

    m0 = jnp.full((1, bq), -jnp.inf, F32)
    l0 = jnp.zeros((1, bq), F32)
    _, l = lax.fori_loop(0, nk, step, (m0, l0), unroll=2)
    o_ref[...] = (acc_ref[...] / l).T.astype(BF16)


def _mla_attention(qt, k, vt, row_off, n_seq, seq_len, bq=1024):
    nq, nk = seq_len // bq, seq_len // MLA_BKV
    assert row_off % seq_len == 0
    q_off, s_off = row_off // bq, row_off // seq_len
    return pl.pallas_call(
        functools.partial(_flash_kernel, nk=nk),
        grid=(n_seq, MLA_HEADS, nq),
        in_specs=[
            pl.BlockSpec((MLA_HW, bq), lambda b, h, qi: (h, q_off + b * nq + qi)),
            pl.BlockSpec((seq_len, MLA_HW), lambda b, h, qi: (s_off + b, h)),
            pl.BlockSpec((1, nk, V_HEAD, MLA_BKV), lambda b, h, qi: (h, s_off + b, 0, 0)),
        ],
        out_specs=pl.BlockSpec((bq, V_HEAD), lambda b, h, qi: (b * nq + qi, h)),
        out_shape=jax.ShapeDtypeStruct((n_seq * seq_len, MLA_HEADS * V_HEAD), BF16),
        scratch_shapes=[pltpu.VMEM((V_HEAD, bq), F32)],
        compiler_params=_params(3),
        name="mla_flash",
    )(qt, k, vt)


def _merge_kernel(ona_ref, omla_ref, wna_ref, wmla_ref, gna_ref, gmla_ref, o_ref):
    pn = jnp.dot(ona_ref[...], wna_ref[...], preferred_element_type=F32)
    pm = jnp.dot(omla_ref[...], wmla_ref[...], preferred_element_type=F32)
    o_ref[...] = (jax.nn.sigmoid(gna_ref[...].astype(F32)) * pn
                  + jax.nn.sigmoid(gmla_ref[...].astype(F32)) * pm).astype(BF16)


def _merge(o_na, o_mla, w_na, w_mla, gates):
    bm, bn = 512, 1024
    return pl.pallas_call(
        _merge_kernel,
        grid=(T // bm, D // bn),
        in_specs=[pl.BlockSpec((bm, NA_WIDTH), lambda i, j: (i, 0)),
                  pl.BlockSpec((bm, MLA_HEADS * V_HEAD), lambda i, j: (i, 0)),
                  pl.BlockSpec((NA_WIDTH, bn), lambda i, j: (0, j)),
                  pl.BlockSpec((MLA_HEADS * V_HEAD, bn), lambda i, j: (0, j)),
                  pl.BlockSpec((bm, bn), lambda i, j: (i, j)),
                  pl.BlockSpec((bm, bn), lambda i, j: (i, D // bn + j))],
        out_specs=pl.BlockSpec((bm, bn), lambda i, j: (i, j)),
        out_shape=jax.ShapeDtypeStruct((T, D), BF16),
        compiler_params=_params(2),
        name="merge_proj",
    )(o_na, o_mla, w_na, w_mla, gates, gates)


def _residual_epilogue(o_ref, x_ref, mod_ref, g_ref, gate_idx):
    y = o_ref[...]
    ms = jnp.mean(y * y, axis=-1, keepdims=True)
    n = y * lax.rsqrt(ms + EPS) * g_ref[...]
    gate = mod_ref[0, gate_idx:gate_idx + 1, :]
    o_ref[...] = x_ref[...] + gate * n


def _out_proj_kernel(a_ref, w_ref, x_ref, mod_ref, g_ref, o_ref, *, gate_idx):
    k = pl.program_id(1)
    part = jnp.dot(a_ref[...], w_ref[...], preferred_element_type=F32)

    @pl.when(k == 0)
    def _():
        o_ref[...] = part

    @pl.when(k > 0)
    def _():
        o_ref[...] += part

    @pl.when(k == pl.num_programs(1) - 1)
    def _():
        _residual_epilogue(o_ref, x_ref, mod_ref, g_ref, gate_idx)


def _out_proj(a, w, x, mod3, g, gate_idx):
    bm, bk = 256, 1024
    kdim = a.shape[1]
    return pl.pallas_call(
        functools.partial(_out_proj_kernel, gate_idx=gate_idx),
        grid=(T // bm, kdim // bk),
        in_specs=[pl.BlockSpec((bm, bk), lambda i, k: (i, k)),
                  pl.BlockSpec((bk, D), lambda i, k: (k, 0)),
                  pl.BlockSpec((bm, D), lambda i, k: (i, 0), pipeline_mode=pl.Buffered(1)),
                  pl.BlockSpec((1, N_MOD, D), lambda i, k: (_seq_of_block(i, bm), 0, 0)),
                  pl.BlockSpec((1, D), lambda i, k: (0, 0))],
        out_specs=pl.BlockSpec((bm, D), lambda i, k: (i, 0)),
        out_shape=jax.ShapeDtypeStruct((T, D), F32),
        compiler_params=_params(2),
        name="out_proj",
    )(a, w, x, mod3, g.reshape(1, D))


def _gelu_tanh(x):
    c = float(np.sqrt(2.0 / np.pi))
    return 0.5 * x * (1.0 + jnp.tanh(c * (x + 0.044715 * (x * x * x))))


def _ffn_down_kernel(a_ref, ap_ref, an_ref, u_ref, cw_ref, cb_ref, w_ref,
                     x_ref, mod_ref, g_ref, o_ref, *, bm, gate_idx):
    i = pl.program_id(0)
    k = pl.program_id(1)
    t0 = i * bm
    t1 = t0 + bm
    seq_start = (t0 == 0) | ((t0 >= T_S) & ((t0 - T_S) % L_P == 0))
    seq_end = (t1 == T_S) | ((t1 > T_S) & ((t1 - T_S) % L_P == 0))
    a = a_ref[...].astype(F32)
    prev_row = ap_ref[BF16_SUBLANES - 1:BF16_SUBLANES, :].astype(F32)
    next_row = an_ref[0:1, :].astype(F32)
    prev_row = jnp.where(seq_start, 0.0, prev_row)
    next_row = jnp.where(seq_end, 0.0, next_row)
    row = lax.broadcasted_iota(jnp.int32, a.shape, 0)
    a_dn = jnp.where(row == 0, prev_row, pltpu.roll(a, 1, 0))
    a_up = jnp.where(row == bm - 1, next_row, pltpu.roll(a, bm - 1, 0))
    conv = a_dn * cw_ref[0:1, :] + a * cw_ref[1:2, :] + a_up * cw_ref[2:3, :] + cb_ref[...]
    gated = (_gelu_tanh(conv) * u_ref[...].astype(F32)).astype(BF16)
    part = jnp.dot(gated, w_ref[...], preferred_element_type=F32)

    @pl.when(k == 0)
    def _():
        o_ref[...] = part

    @pl.when(k > 0)
    def _():
        o_ref[...] += part

    @pl.when(k == pl.num_programs(1) - 1)
    def _():
        _residual_epilogue(o_ref, x_ref, mod_ref, g_ref, gate_idx)


def _ffn_down(au, conv_w, conv_b, w_down, x, mod3, g, gate_idx):
    bm, bk = 256, 1024
    nk = D_FF_PAD // bk
    halo = BF16_SUBLANES
    last_halo = T // halo - 1
    return pl.pallas_call(
        functools.partial(_ffn_down_kernel, bm=bm, gate_idx=gate_idx),
        grid=(T // bm, nk),
        in_specs=[
            pl.BlockSpec((bm, bk), lambda i, k: (i, k)),
            pl.BlockSpec((halo, bk), lambda i, k: (jnp.maximum(i * (bm // halo) - 1, 0), k)),
            pl.BlockSpec((halo, bk), lambda i, k: (jnp.minimum((i + 1) * (bm // halo), last_halo), k)),
            pl.BlockSpec((bm, bk), lambda i, k: (i, nk + k)),
            pl.BlockSpec((3, bk), lambda i, k: (0, k)),
            pl.BlockSpec((1, bk), lambda i, k: (0, k)),
            pl.BlockSpec((bk, D), lambda i, k: (k, 0)),
            pl.BlockSpec((bm, D), lambda i, k: (i, 0), pipeline_mode=pl.Buffered(1)),
            pl.BlockSpec((1, N_MOD, D), lambda i, k: (_seq_of_block(i, bm), 0, 0)),
            pl.BlockSpec((1, D), lambda i, k: (0, 0)),
        ],
        out_specs=pl.BlockSpec((bm, D), lambda i, k: (i, 0)),
        out_shape=jax.ShapeDtypeStruct((T, D), F32),
        compiler_params=_params(2),
        name="ffn_down",
    )(au, au, au, au, conv_w, conv_b, w_down, x, mod3, g.reshape(1, D))


def _prep_weights(w_in, w_uq, w_ukv, w_ffn_in, conv_w, conv_b, w_ffn_down):
    c_qkv = 3 * NA_WIDTH
    c_q = c_qkv + Q_LORA
    c_kv = c_q + KV_LORA
    c_r = c_kv + QK_ROPE
    w_qkv = w_in[:, :c_qkv].astype(BF16)
    w_lat = jnp.zeros((D, LAT_W), F32)
    w_lat = w_lat.at[:, LAT_Q:LAT_Q + Q_LORA].set(w_in[:, c_qkv:c_q])
    w_lat = w_lat.at[:, LAT_KV:LAT_KV + KV_LORA].set(w_in[:, c_q:c_kv])
    w_lat = w_lat.at[:, LAT_ROPE:LAT_ROPE + QK_ROPE].set(w_in[:, c_kv:c_r])
    w_lat = w_lat.astype(BF16)
    w_gate = w_in[:, c_r:].astype(BF16)
    uq = w_uq.reshape(Q_LORA, MLA_HEADS, QK_NOPE + QK_ROPE)
    uq = jnp.concatenate([uq, jnp.zeros((Q_LORA, MLA_HEADS, LANES - QK_ROPE), F32)], axis=2)
    w_uq_t = uq.reshape(Q_LORA, MLA_HEADS * MLA_HW).T.astype(BF16)
    ukv = w_ukv.reshape(KV_LORA, MLA_HEADS, QK_NOPE + V_HEAD)
    w_k = ukv[:, :, :QK_NOPE].reshape(KV_LORA, MLA_HEADS * QK_NOPE).astype(BF16)
    w_vt = ukv[:, :, QK_NOPE:].reshape(KV_LORA, MLA_HEADS * V_HEAD).T.astype(BF16)
    pad = D_FF_PAD - D_FF
    w_a = jnp.pad(w_ffn_in[:, :D_FF], ((0, 0), (0, pad)))
    w_u = jnp.pad(w_ffn_in[:, D_FF:], ((0, 0), (0, pad)))
    w_au = jnp.concatenate([w_a, w_u], axis=1).astype(BF16)
    cw = jnp.pad(conv_w, ((0, 0), (0, pad)))
    cb = jnp.pad(conv_b, ((0, pad),)).reshape(1, D_FF_PAD)
    w_down = jnp.pad(w_ffn_down, ((0, pad), (0, 0))).astype(BF16)
    return w_qkv, w_lat, w_gate, w_uq_t, w_k, w_vt, w_au, cw, cb, w_down


def kernel(x_prompt, x_sample, c_prompt, c_sample, w_ada, b_ada, g_pre_mix, g_post_mix, w_in, rpb,
           g_q, w_uq, g_kv, w_ukv, w_na_proj, w_mla_proj, w_out, g_pre_ffn, g_post_ffn,
           w_ffn_in, conv_w, conv_b, w_ffn_down):
    assert w_ada.shape[0] == 1
    x = jnp.concatenate([x_sample.reshape(T_S, D), x_prompt.reshape(T_P, D)], axis=0)
    c8 = jnp.concatenate([c_sample, c_prompt, jnp.zeros((8 - N_SEQ, D), F32)], axis=0)
    (w_qkv, w_lat, w_gate, w_uq_t, w_k, w_vt, w_au, cw, cb, w_down) = _prep_weights(
        w_in[0], w_uq[0], w_ukv[0], w_ffn_in[0], conv_w[0], conv_b[0], w_ffn_down[0])

    mod3 = _ada(c8, w_ada[0], b_ada[0]).reshape(8, N_MOD, D)

    h1 = _norm_mod(x, g_pre_mix[0], mod3, 0, 1)
    qkv = _matmul(h1, w_qkv, BF16, 1024, 1024, "in_proj_qkv")
    lat = _matmul(h1, w_lat, F32, 1024, 1024, "in_proj_lat")
    gates = _matmul(h1, w_gate, BF16, 1024, 1024, "in_proj_gate")

    bias = _na_bias(rpb[0])
    o_na = jnp.concatenate([_na_attention(qkv, bias, 0, 1, L_S),
                            _na_attention(qkv, bias, T_S, B_P, L_P)], axis=0)

    tabs, tabs_t = _rope_tables()
    qt = _uq(lat, g_q[0], w_uq_t, tabs_t)
    k, vt = _ukv(lat, g_kv[0], w_k, w_vt, tabs)
    o_mla = jnp.concatenate([_mla_attention(qt, k, vt, 0, 1, L_S),
                             _mla_attention(qt, k, vt, T_S, B_P, L_P)], axis=0)

    merged = _merge(o_na, o_mla, w_na_proj[0].astype(BF16), w_mla_proj[0].astype(BF16), gates)
    x1 = _out_proj(merged, w_out[0].astype(BF16), x, mod3, g_post_mix[0], 2)

    h2 = _norm_mod(x1, g_pre_ffn[0], mod3, 3, 4)
    au = _matmul(h2, w_au, BF16, 1024, 1024, "ffn_in")
    out = _out_ffn(au, cw, cb, w_down, x1, mod3, g_post_ffn[0])

    y_sample = out[:T_S].reshape(1, L_S, D)
    y_prompt = out[T_S:].reshape(B_P, L_P, D)
    return (y_prompt, y_sample)


def _out_ffn(au, cw, cb, w_down, x1, mod3, g):
    return _ffn_down(au, cw, cb, w_down, x1, mod3, g, 5)
```

```python
import functools

import numpy as np
import jax
import jax.numpy as jnp
from jax import lax
from jax.experimental import pallas as pl
from jax.experimental.pallas import tpu as pltpu

F32 = jnp.float32
BF16 = jnp.bfloat16

D = 4096
B_P, L_P = 4, 2048
L_S = 16384
T_S = L_S
T_P = B_P * L_P
T = T_S + T_P
N_SEQ = 1 + B_P
N_MOD = 6
GRID_W = 64
NA_HEADS, NA_DIM = 16, 128
NA_WIDTH = NA_HEADS * NA_DIM
NA_KH, NA_KW = 8, 16
MLA_HEADS = 16
Q_LORA, KV_LORA = 896, 512
QK_NOPE, QK_ROPE, V_HEAD = 128, 64, 128
ROPE_THETA = 10000.0
D_FF = 11008
D_FF_PAD = 11264
EPS = 1e-6
NEG = -1e30
LOG2E = float(np.log2(np.e))

LAT_W = 2048
LAT_Q, LAT_KV, LAT_ROPE = 0, 1024, 1536

V7X_VMEM_BYTES = 64 * 1024 * 1024
VMEM_LIMIT = V7X_VMEM_BYTES - 8 * 1024 * 1024
LANES = 128
BF16_SUBLANES = 16

NA_QROWS = 8
NA_SLAB_ROWS = 16
NA_BQ = NA_QROWS * GRID_W
NA_BK = NA_SLAB_ROWS * GRID_W


def _params(n_axes):
    return pltpu.CompilerParams(dimension_semantics=("arbitrary",) * n_axes,
                                vmem_limit_bytes=VMEM_LIMIT)


def _seq_of_block(i, bm):
    t0 = i * bm
    return jnp.where(t0 < T_S, 0, 1 + jnp.maximum(t0 - T_S, 0) // L_P)


def _ada_kernel(c_ref, w_ref, b_ref, o_ref):
    c = c_ref[...]
    s = c * jax.nn.sigmoid(c)
    o_ref[...] = jnp.dot(s.astype(BF16), w_ref[...].astype(BF16),
                         preferred_element_type=F32) + b_ref[...]


def _ada(c8, w_ada, b_ada):
    bn = 512
    n = N_MOD * D
    return pl.pallas_call(
        _ada_kernel,
        grid=(n // bn,),
        in_specs=[pl.BlockSpec((8, D), lambda j: (0, 0)),
                  pl.BlockSpec((D, bn), lambda j: (0, j)),
                  pl.BlockSpec((1, bn), lambda j: (0, j))],
        out_specs=pl.BlockSpec((8, bn), lambda j: (0, j)),
        out_shape=jax.ShapeDtypeStruct((8, n), F32),
        compiler_params=_params(1),
        name="ada_mod",
    )(c8, w_ada, b_ada.reshape(1, n))


def _norm_mod_kernel(xs_ref, xp_ref, g_ref, mod_ref, o_ref, *, n_s):
    x = jnp.where(pl.program_id(0) < n_s, xs_ref[...], xp_ref[...])
    ms = jnp.mean(x * x, axis=-1, keepdims=True)
    y = x * lax.rsqrt(ms + EPS) * g_ref[...]
    o_ref[...] = (y * (1.0 + mod_ref[0, 1:2, :]) + mod_ref[0, 0:1, :]).astype(BF16)


def _norm_mod(xs, xp, g, mod3):
    bm = 256
    n_s = T_S // bm
    return pl.pallas_call(
        functools.partial(_norm_mod_kernel, n_s=n_s),
        grid=(T // bm,),
        in_specs=[pl.BlockSpec((bm, D), lambda i: (jnp.minimum(i, n_s - 1), 0)),
                  pl.BlockSpec((bm, D), lambda i: (jnp.maximum(i - n_s, 0), 0)),
                  pl.BlockSpec((1, D), lambda i: (0, 0)),
                  pl.BlockSpec((1, N_MOD, D), lambda i: (_seq_of_block(i, bm), 0, 0))],
        out_specs=pl.BlockSpec((bm, D), lambda i: (i, 0)),
        out_shape=jax.ShapeDtypeStruct((T, D), BF16),
        compiler_params=_params(1),
        name="norm_mod",
    )(xs, xp, g.reshape(1, D), mod3)


def _mm_kernel(a_ref, w_ref, o_ref):
    o_ref[...] = jnp.dot(a_ref[...], w_ref[...],
                         preferred_element_type=F32).astype(o_ref.dtype)


def _matmul(a, w, out_dtype, bm, bn, name):
    m, k = a.shape
    _, n = w.shape
    return pl.pallas_call(
        _mm_kernel,
        grid=(m // bm, n // bn),
        in_specs=[pl.BlockSpec((bm, k), lambda i, j: (i, 0)),
                  pl.BlockSpec((k, bn), lambda i, j: (0, j))],
        out_specs=pl.BlockSpec((bm, bn), lambda i, j: (i, j)),
        out_shape=jax.ShapeDtypeStruct((m, n), out_dtype),
        compiler_params=_params(2),
        name=name,
    )(a, w)


def _na_case_geometry(case, qr, kr):
    if case == 0:
        start = max(qr - NA_KH // 2, 0)
        return start <= kr < start + NA_KH, kr - qr + NA_KH - 1
    if case == 1:
        return qr <= kr < qr + NA_KH, kr - qr + NA_KH - 1 - NA_KH // 2
    lo = NA_QROWS + min(qr - NA_KH // 2, 0)
    return lo <= kr < lo + NA_KH, kr - qr - 1


def _na_bias_kernel(rpb_ref, o_ref):
    h = pl.program_id(0)
    n_dr, n_dc = 2 * NA_KH - 1, 2 * NA_KW - 1
    qc = lax.broadcasted_iota(jnp.int32, (GRID_W, LANES), 0)
    lane = lax.broadcasted_iota(jnp.int32, (GRID_W, LANES), 1)
    kc = lane % GRID_W
    cstart = jnp.clip(qc - NA_KW // 2, 0, GRID_W - NA_KW)
    col_valid = (kc >= cstart) & (kc < cstart + NA_KW)
    dc_idx = jnp.clip(kc - qc + NA_KW - 1, 0, n_dc - 1)
    neg = jnp.full((GRID_W, LANES), NEG, F32)
    tiles = []
    for i in range(n_dr):
        acc = jnp.zeros((GRID_W, LANES), F32)
        for j in range(n_dc):
            acc = jnp.where(dc_idx == j, rpb_ref[h * (n_dr * n_dc) + i * n_dc + j], acc)
        tiles.append(jnp.where(col_valid, acc * LOG2E, neg))
    left = lane < GRID_W
    for case in range(3):
        for qr in range(NA_QROWS):
            for kp in range(NA_SLAB_ROWS // 2):
                v0, i0 = _na_case_geometry(case, qr, 2 * kp)
                v1, i1 = _na_case_geometry(case, qr, 2 * kp + 1)
                t0 = tiles[i0] if v0 else neg
                t1 = tiles[i1] if v1 else neg
                if v0 or v1:
                    blk = jnp.where(left, t0, t1)
                else:
                    blk = neg
                o_ref[case, 0, qr * GRID_W:(qr + 1) * GRID_W,
                      kp * LANES:(kp + 1) * LANES] = blk


def _na_bias(rpb):
    for case in range(3):
        for qr in range(NA_QROWS):
            for kr in range(NA_SLAB_ROWS):
                valid, idx = _na_case_geometry(case, qr, kr)
                assert not valid or 0 <= idx < 2 * NA_KH - 1
    return pl.pallas_call(
        _na_bias_kernel,
        grid=(NA_HEADS,),
        in_specs=[pl.BlockSpec(memory_space=pltpu.SMEM)],
        out_specs=pl.BlockSpec((3, 1, NA_BQ, NA_BK), lambda h: (0, h, 0, 0)),
        out_shape=jax.ShapeDtypeStruct((3, NA_HEADS, NA_BQ, NA_BK), F32),
        compiler_params=_params(1),
        name="na_bias",
    )(rpb.reshape(-1))


def _na_kernel(q_ref, k_ref, v_ref, b0_ref, b1_ref, o_ref, *, n_rb, rows):
    pair = pl.program_id(1) % (n_rb // 2)
    for half, b_ref in enumerate((b0_ref, b1_ref)):
        rb = 2 * pair + half
        slab = jnp.clip(rb * NA_QROWS - NA_KH // 2, 0, rows - NA_SLAB_ROWS) * GRID_W
        slab = pl.multiple_of(slab, NA_KH // 2 * GRID_W)
        ks = k_ref[pl.ds(slab, NA_BK), :]
        vs = v_ref[pl.ds(slab, NA_BK), :]
        s = lax.dot_general(q_ref[half * NA_BQ:(half + 1) * NA_BQ, :], ks,
                            (((1,), (1,)), ((), ())), preferred_element_type=F32)
        s = s + b_ref[0, 0]
        m = jnp.max(s, axis=1, keepdims=True)
        p = jnp.exp2(s - m)
        l = jnp.sum(p, axis=1, keepdims=True)
        o = jnp.dot(p.astype(BF16), vs, preferred_element_type=F32)
        o_ref[half * NA_BQ:(half + 1) * NA_BQ, :] = (o / l).astype(BF16)


def _na_attention(qkv, bias, row_off, n_seq, seq_len):
    rows = seq_len // GRID_W
    n_rb = rows // NA_QROWS
    assert rows >= NA_SLAB_ROWS and row_off % seq_len == 0 and n_rb % 2 == 0
    n_pair = n_rb // 2
    q_off = row_off // (2 * NA_BQ)
    s_off = row_off // seq_len

    def case_of(i, half):
        rb = 2 * (i % n_pair) + half
        return jnp.where(rb == 0, 0, jnp.where(rb == n_rb - 1, 2, 1))

    return pl.pallas_call(
        functools.partial(_na_kernel, n_rb=n_rb, rows=rows),
        grid=(NA_HEADS, n_seq * n_pair),
        in_specs=[
            pl.BlockSpec((2 * NA_BQ, NA_DIM), lambda h, i: (q_off + i, h)),
            pl.BlockSpec((seq_len, NA_DIM), lambda h, i: (s_off + i // n_pair, NA_HEADS + h)),
            pl.BlockSpec((seq_len, NA_DIM), lambda h, i: (s_off + i // n_pair, 2 * NA_HEADS + h)),
            pl.BlockSpec((1, 1, NA_BQ, NA_BK), lambda h, i: (case_of(i, 0), h, 0, 0)),
            pl.BlockSpec((1, 1, NA_BQ, NA_BK), lambda h, i: (case_of(i, 1), h, 0, 0)),
        ],
        out_specs=pl.BlockSpec((2 * NA_BQ, NA_DIM), lambda h, i: (i, h)),
        out_shape=jax.ShapeDtypeStruct((n_seq * seq_len, NA_WIDTH), BF16),
        compiler_params=_params(2),
        name="na_attn",
    )(qkv, qkv, qkv, bias, bias)


def _rope_tables():
    half = QK_ROPE // 2
    inv = jnp.power(ROPE_THETA, -jnp.arange(0, QK_ROPE, 2, dtype=F32) / QK_ROPE)
    pos = jnp.concatenate([jnp.arange(L_S, dtype=F32),
                           jnp.tile(jnp.arange(L_P, dtype=F32), B_P)])
    ang = pos[:, None] * inv[None, :]
    cos, sin = jnp.cos(ang), jnp.sin(ang)
    z = jnp.zeros((T, LANES - QK_ROPE), F32)
    zh = jnp.zeros((T, half), F32)
    c_tab = jnp.concatenate([cos, cos, z], axis=1)
    sp_tab = jnp.concatenate([zh, sin, z], axis=1)
    sm_tab = jnp.concatenate([-sin, zh, z], axis=1)
    return (c_tab, sp_tab, sm_tab), (cos.T, sin.T)


def _apply_rope(r, c_ref, sp_ref, sm_ref):
    half = QK_ROPE // 2
    return (r * c_ref[...] + pltpu.roll(r, half, 1) * sp_ref[...]
            + pltpu.roll(r, LANES - half, 1) * sm_ref[...])


MLA_HW = 2 * LANES
MLA_BKV = 1024


def _rms_norm_bf16(x, g):
    ms = jnp.mean(x * x, axis=-1, keepdims=True)
    return (x * lax.rsqrt(ms + EPS) * g).astype(BF16)


def _uq_kernel(x_ref, g_ref, w_ref, cos_ref, sin_ref, o_ref):
    xn = _rms_norm_bf16(x_ref[...], g_ref[...])
    acc = lax.dot_general(w_ref[...], xn, (((1,), (1,)), ((), ())),
                          preferred_element_type=F32)
    scale = (QK_NOPE + QK_ROPE) ** -0.5 * LOG2E
    half = QK_ROPE // 2
    cos = cos_ref[...] * scale
    sin = sin_ref[...] * scale
    for h in range(MLA_HEADS):
        r0 = h * MLA_HW
        o_ref[r0:r0 + QK_NOPE, :] = (acc[r0:r0 + QK_NOPE, :] * scale).astype(BF16)
        x1 = acc[r0 + QK_NOPE:r0 + QK_NOPE + half, :]
        x2 = acc[r0 + QK_NOPE + half:r0 + QK_NOPE + QK_ROPE, :]
        o_ref[r0 + QK_NOPE:r0 + QK_NOPE + half, :] = (x1 * cos - x2 * sin).astype(BF16)
        o_ref[r0 + QK_NOPE + half:r0 + QK_NOPE + QK_ROPE, :] = (x1 * sin + x2 * cos).astype(BF16)
        o_ref[r0 + QK_NOPE + QK_ROPE:r0 + MLA_HW, :] = jnp.zeros(
            (MLA_HW - QK_NOPE - QK_ROPE, o_ref.shape[1]), BF16)


def _uq(lat, g_q, w_uq_t, tabs_t):
    bm = 512
    half = QK_ROPE // 2
    tab_spec = pl.BlockSpec((half, bm), lambda i: (0, i))
    return pl.pallas_call(
        _uq_kernel,
        grid=(T // bm,),
        in_specs=[pl.BlockSpec((bm, Q_LORA), lambda i: (i, LAT_Q // Q_LORA)),
                  pl.BlockSpec((1, Q_LORA), lambda i: (0, 0)),
                  pl.BlockSpec((MLA_HEADS * MLA_HW, Q_LORA), lambda i: (0, 0)),
                  tab_spec, tab_spec],
        out_specs=pl.BlockSpec((MLA_HEADS * MLA_HW, bm), lambda i: (0, i)),
        out_shape=jax.ShapeDtypeStruct((MLA_HEADS * MLA_HW, T), BF16),
        compiler_params=_params(1),
        name="mla_uq",
    )(lat, g_q.reshape(1, Q_LORA), w_uq_t, *tabs_t)


def _ukv_kernel(x_ref, r_ref, g_ref, wk_ref, wv_ref, c_ref, sp_ref, sm_ref, k_ref, vt_ref):
    xn = _rms_norm_bf16(x_ref[...], g_ref[...])
    kn = jnp.dot(xn, wk_ref[...], preferred_element_type=F32)
    kpe = _apply_rope(r_ref[...], c_ref, sp_ref, sm_ref).astype(BF16)
    vt = lax.dot_general(wv_ref[...], xn, (((1,), (1,)), ((), ())),
                         preferred_element_type=F32)
    for h in range(MLA_HEADS):
        k_ref[:, h * MLA_HW:h * MLA_HW + QK_NOPE] = kn[:, h * QK_NOPE:(h + 1) * QK_NOPE].astype(BF16)
        k_ref[:, h * MLA_HW + QK_NOPE:(h + 1) * MLA_HW] = kpe
        vt_ref[h, 0] = vt[h * V_HEAD:(h + 1) * V_HEAD, :].astype(BF16)


def _ukv(lat, g_kv, w_k, w_vt, tabs):
    bm = MLA_BKV
    tab_spec = pl.BlockSpec((bm, LANES), lambda i: (i, 0))
    return pl.pallas_call(
        _ukv_kernel,
        grid=(T // bm,),
        in_specs=[pl.BlockSpec((bm, KV_LORA), lambda i: (i, LAT_KV // KV_LORA)),
                  pl.BlockSpec((bm, LANES), lambda i: (i, LAT_ROPE // LANES)),
                  pl.BlockSpec((1, KV_LORA), lambda i: (0, 0)),
                  pl.BlockSpec((KV_LORA, MLA_HEADS * QK_NOPE), lambda i: (0, 0)),
                  pl.BlockSpec((MLA_HEADS * V_HEAD, KV_LORA), lambda i: (0, 0)),
                  tab_spec, tab_spec, tab_spec],
        out_specs=[pl.BlockSpec((bm, MLA_HEADS * MLA_HW), lambda i: (i, 0)),
                   pl.BlockSpec((MLA_HEADS, 1, V_HEAD, bm), lambda i: (0, i, 0, 0))],
        out_shape=[jax.ShapeDtypeStruct((T, MLA_HEADS * MLA_HW), BF16),
                   jax.ShapeDtypeStruct((MLA_HEADS, T // bm, V_HEAD, bm), BF16)],
        compiler_params=_params(1),
        name="mla_ukv",
    )(lat, lat, g_kv.reshape(1, KV_LORA), w_k, w_vt, *tabs)


def _flash_kernel(qt_ref, k_ref, vt_ref, o_ref, acc_ref, s_ref, *, nk):
    bq = qt_ref.shape[1]
    acc_ref[...] = jnp.zeros(acc_ref.shape, F32)

    def scores(j, slot):
        start = pl.multiple_of(j * MLA_BKV, MLA_BKV)
        s_ref[slot] = jnp.dot(k_ref[pl.ds(start, MLA_BKV), :], qt_ref[...],
                              preferred_element_type=F32)

    def update(j, slot, m_prev, l_prev):
        s = s_ref[slot]
        m_new = jnp.maximum(m_prev, jnp.max(s, axis=0, keepdims=True))
        alpha = jnp.exp2(m_prev - m_new)
        p = jnp.exp2(s - m_new)
        l_new = alpha * l_prev + jnp.sum(p, axis=0, keepdims=True)
        acc_ref[...] = alpha * acc_ref[...] + jnp.dot(
            vt_ref[0, j], p.astype(BF16), preferred_element_type=F32)
        return m_new, l_new

    def pair(jj, carry):
        m, l = carry
        j0 = 2 * jj
        scores(j0 + 1, 1)
        m, l = update(j0, 0, m, l)
        scores(jnp.minimum(j0 + 2, nk - 1), 0)
        m, l = update(j0 + 1, 1, m, l)
        return m, l

    scores(0, 0)
    m0 = jnp.full((1, bq), -jnp.inf, F32)
    l0 = jnp.zeros((1, bq), F32)
    _, l = lax.fori_loop(0, nk // 2, pair, (m0, l0))
    o_ref[...] = (acc_ref[...] / l).T.astype(BF16)


def _mla_attention(qt, k, vt, row_off, n_seq, seq_len, bq=1024):
    nq, nk = seq_len // bq, seq_len // MLA_BKV
    assert row_off % seq_len == 0 and nk % 2 == 0
    q_off, s_off = row_off // bq, row_off // seq_len
    return pl.pallas_call(
        functools.partial(_flash_kernel, nk=nk),
        grid=(n_seq, MLA_HEADS, nq),
        in_specs=[
            pl.BlockSpec((MLA_HW, bq), lambda b, h, qi: (h, q_off + b * nq + qi)),
            pl.BlockSpec((seq_len, MLA_HW), lambda b, h, qi: (s_off + b, h)),
            pl.BlockSpec((1, nk, V_HEAD, MLA_BKV), lambda b, h, qi: (h, s_off + b, 0, 0)),
        ],
        out_specs=pl.BlockSpec((bq, V_HEAD), lambda b, h, qi: (b * nq + qi, h)),
        out_shape=jax.ShapeDtypeStruct((n_seq * seq_len, MLA_HEADS * V_HEAD), BF16),
        scratch_shapes=[pltpu.VMEM((V_HEAD, bq), F32),
                        pltpu.VMEM((2, MLA_BKV, bq), F32)],
        compiler_params=_params(3),
        name="mla_flash",
    )(qt, k, vt)


def _merge_kernel(nas_ref, nap_ref, mlas_ref, mlap_ref, wna_ref, wmla_ref, gna_ref, gmla_ref,
                  o_ref, *, n_s):
    is_sample = pl.program_id(0) < n_s
    o_na = jnp.where(is_sample, nas_ref[...], nap_ref[...])
    o_mla = jnp.where(is_sample, mlas_ref[...], mlap_ref[...])
    pn = jnp.dot(o_na, wna_ref[...], preferred_element_type=F32)
    pm = jnp.dot(o_mla, wmla_ref[...], preferred_element_type=F32)
    o_ref[...] = (jax.nn.sigmoid(gna_ref[...].astype(F32)) * pn
                  + jax.nn.sigmoid(gmla_ref[...].astype(F32)) * pm).astype(BF16)


def _merge(na_s, na_p, mla_s, mla_p, w_na, w_mla, gates):
    bm, bn = 512, 1024
    n_s = T_S // bm
    width = NA_WIDTH
    assert width == MLA_HEADS * V_HEAD
    s_spec = pl.BlockSpec((bm, width), lambda i, j: (jnp.minimum(i, n_s - 1), 0))
    p_spec = pl.BlockSpec((bm, width), lambda i, j: (jnp.maximum(i - n_s, 0), 0))
    return pl.pallas_call(
        functools.partial(_merge_kernel, n_s=n_s),
        grid=(T // bm, D // bn),
        in_specs=[s_spec, p_spec, s_spec, p_spec,
                  pl.BlockSpec((width, bn), lambda i, j: (0, j)),
                  pl.BlockSpec((width, bn), lambda i, j: (0, j)),
                  pl.BlockSpec((bm, bn), lambda i, j: (i, j)),
                  pl.BlockSpec((bm, bn), lambda i, j: (i, D // bn + j))],
        out_specs=pl.BlockSpec((bm, bn), lambda i, j: (i, j)),
        out_shape=jax.ShapeDtypeStruct((T, D), BF16),
        compiler_params=_params(2),
        name="merge_proj",
    )(na_s, na_p, mla_s, mla_p, w_na, w_mla, gates, gates)


def _mm_kacc_kernel(a_ref, w_ref, o_ref):
    part = jnp.dot(a_ref[...], w_ref[...], preferred_element_type=F32)

    @pl.when(pl.program_id(2) == 0)
    def _():
        o_ref[...] = part

    @pl.when(pl.program_id(2) > 0)
    def _():
        o_ref[...] += part


def _matmul_kacc(a, w, bm, bn, bk, name):
    m, kdim = a.shape
    _, n = w.shape
    return pl.pallas_call(
        _mm_kacc_kernel,
        grid=(m // bm, n // bn, kdim // bk),
        in_specs=[pl.BlockSpec((bm, bk), lambda i, j, k: (i, k)),
                  pl.BlockSpec((bk, bn), lambda i, j, k: (k, j))],
        out_specs=pl.BlockSpec((bm, bn), lambda i, j, k: (i, j)),
        out_shape=jax.ShapeDtypeStruct((m, n), F32),
        compiler_params=_params(3),
        name=name,
    )(a, w)


RES_BM = 256
N_S_BLOCKS = T_S // RES_BM


def _gated_residual(y, x, mod_ref, g_ref, gate_idx):
    ms = jnp.mean(y * y, axis=-1, keepdims=True)
    n = y * lax.rsqrt(ms + EPS) * g_ref[...]
    return x + mod_ref[0, gate_idx:gate_idx + 1, :] * n


def _resid_mix_kernel(y_ref, xs_ref, xp_ref, mod_ref, gpost_ref, gpre_ref, x1_ref, h_ref):
    x = jnp.where(pl.program_id(0) < N_S_BLOCKS, xs_ref[...], xp_ref[...])
    x1 = _gated_residual(y_ref[...], x, mod_ref, gpost_ref, 2)
    x1_ref[...] = x1
    ms = jnp.mean(x1 * x1, axis=-1, keepdims=True)
    n = x1 * lax.rsqrt(ms + EPS) * gpre_ref[...]
    h_ref[...] = (n * (1.0 + mod_ref[0, 4:5, :]) + mod_ref[0, 3:4, :]).astype(BF16)


def _split_row_specs(bm):
    n_s = T_S // bm
    return (pl.BlockSpec((bm, D), lambda i: (jnp.minimum(i, n_s - 1), 0)),
            pl.BlockSpec((bm, D), lambda i: (jnp.maximum(i - n_s, 0), 0)))


def _resid_mix(y, xs, xp, mod3, g_post, g_pre):
    bm = RES_BM
    row = pl.BlockSpec((bm, D), lambda i: (i, 0))
    vec = pl.BlockSpec((1, D), lambda i: (0, 0))
    s_spec, p_spec = _split_row_specs(bm)
    return pl.pallas_call(
        _resid_mix_kernel,
        grid=(T // bm,),
        in_specs=[row, s_spec, p_spec,
                  pl.BlockSpec((1, N_MOD, D), lambda i: (_seq_of_block(i, bm), 0, 0)),
                  vec, vec],
        out_specs=[row, row],
        out_shape=[jax.ShapeDtypeStruct((T, D), F32), jax.ShapeDtypeStruct((T, D), BF16)],
        compiler_params=_params(1),
        name="resid_mix",
    )(y, xs, xp, mod3, g_post.reshape(1, D), g_pre.reshape(1, D))


def _resid_ffn_kernel(f_ref, x_ref, mod_ref, g_ref, os_ref, op_ref):
    out = _gated_residual(f_ref[...], x_ref[...], mod_ref, g_ref, 5)

    @pl.when(pl.program_id(0) < N_S_BLOCKS)
    def _():
        os_ref[...] = out

    @pl.when(pl.program_id(0) >= N_S_BLOCKS)
    def _():
        op_ref[...] = out


def _resid_ffn(f, x1, mod3, g_post):
    bm = RES_BM
    row = pl.BlockSpec((bm, D), lambda i: (i, 0))
    s_spec, p_spec = _split_row_specs(bm)
    return pl.pallas_call(
        _resid_ffn_kernel,
        grid=(T // bm,),
        in_specs=[row, row,
                  pl.BlockSpec((1, N_MOD, D), lambda i: (_seq_of_block(i, bm), 0, 0)),
                  pl.BlockSpec((1, D), lambda i: (0, 0))],
        out_specs=[s_spec, p_spec],
        out_shape=[jax.ShapeDtypeStruct((T_S, D), F32), jax.ShapeDtypeStruct((T_P, D), F32)],
        compiler_params=_params(1),
        name="resid_ffn",
    )(f, x1, mod3, g_post.reshape(1, D))


def _glu_kernel(a_ref, ap_ref, an_ref, u_ref, cw_ref, cb_ref, o_ref, *, bm):
    t0 = pl.program_id(0) * bm
    t1 = t0 + bm
    seq_start = (t0 == 0) | ((t0 >= T_S) & ((t0 - T_S) % L_P == 0))
    seq_end = (t1 == T_S) | ((t1 > T_S) & ((t1 - T_S) % L_P == 0))
    a = a_ref[...].astype(F32)
    prev_row = ap_ref[BF16_SUBLANES - 1:BF16_SUBLANES, :].astype(F32)
    next_row = an_ref[0:1, :].astype(F32)
    prev_row = jnp.where(seq_start, 0.0, prev_row)
    next_row = jnp.where(seq_end, 0.0, next_row)
    row = lax.broadcasted_iota(jnp.int32, a.shape, 0)
    a_dn = jnp.where(row == 0, prev_row, pltpu.roll(a, 1, 0))
    a_up = jnp.where(row == bm - 1, next_row, pltpu.roll(a, bm - 1, 0))
    x = a_dn * cw_ref[0:1, :] + a * cw_ref[1:2, :] + a_up * cw_ref[2:3, :] + cb_ref[...]
    c = float(np.sqrt(2.0 / np.pi))
    t = jnp.tanh(x * (c + (c * 0.044715) * (x * x)))
    o_ref[...] = ((x + x * t) * u_ref[...].astype(F32)).astype(BF16)


def _glu(au, conv_w, conv_b):
    bm, bn = 512, 512
    nn = D_FF_PAD // bn
    halo = BF16_SUBLANES
    last_halo = T // halo - 1
    return pl.pallas_call(
        functools.partial(_glu_kernel, bm=bm),
        grid=(T // bm, nn),
        in_specs=[
            pl.BlockSpec((bm, bn), lambda i, j: (i, j)),
            pl.BlockSpec((halo, bn), lambda i, j: (jnp.maximum(i * (bm // halo) - 1, 0), j)),
            pl.BlockSpec((halo, bn), lambda i, j: (jnp.minimum((i + 1) * (bm // halo), last_halo), j)),
            pl.BlockSpec((bm, bn), lambda i, j: (i, nn + j)),
            pl.BlockSpec((3, bn), lambda i, j: (0, j)),
            pl.BlockSpec((1, bn), lambda i, j: (0, j)),
        ],
        out_specs=pl.BlockSpec((bm, bn), lambda i, j: (i, j)),
        out_shape=jax.ShapeDtypeStruct((T, D_FF_PAD), BF16),
        compiler_params=_params(2),
        name="conv_glu",
    )(au, au, au, au, conv_w, conv_b)


def _prep_weights(w_in, w_uq, w_ukv, w_ffn_in, conv_w, conv_b, w_ffn_down):
    c_qkv = 3 * NA_WIDTH
    c_q = c_qkv + Q_LORA
    c_kv = c_q + KV_LORA
    c_r = c_kv + QK_ROPE
    q_scale = jnp.concatenate([jnp.full((NA_WIDTH,), NA_DIM ** -0.5 * LOG2E, F32),
                               jnp.ones((2 * NA_WIDTH,), F32)])
    w_qkv = (w_in[:, :c_qkv] * q_scale[None, :]).astype(BF16)
    w_lat = jnp.zeros((D, LAT_W), F32)
    w_lat = w_lat.at[:, LAT_Q:LAT_Q + Q_LORA].set(w_in[:, c_qkv:c_q])
    w_lat = w_lat.at[:, LAT_KV:LAT_KV + KV_LORA].set(w_in[:, c_q:c_kv])
    w_lat = w_lat.at[:, LAT_ROPE:LAT_ROPE + QK_ROPE].set(w_in[:, c_kv:c_r])
    w_lat = w_lat.astype(BF16)
    w_gate = w_in[:, c_r:].astype(BF16)
    uq = w_uq.reshape(Q_LORA, MLA_HEADS, QK_NOPE + QK_ROPE)
    uq = jnp.concatenate([uq, jnp.zeros((Q_LORA, MLA_HEADS, LANES - QK_ROPE), F32)], axis=2)
    w_uq_t = uq.reshape(Q_LORA, MLA_HEADS * MLA_HW).T.astype(BF16)
    ukv = w_ukv.reshape(KV_LORA, MLA_HEADS, QK_NOPE + V_HEAD)
    w_k = ukv[:, :, :QK_NOPE].reshape(KV_LORA, MLA_HEADS * QK_NOPE).astype(BF16)
    w_vt = ukv[:, :, QK_NOPE:].reshape(KV_LORA, MLA_HEADS * V_HEAD).T.astype(BF16)
    pad = D_FF_PAD - D_FF
    w_a = jnp.pad(w_ffn_in[:, :D_FF], ((0, 0), (0, pad)))
    w_u = jnp.pad(w_ffn_in[:, D_FF:], ((0, 0), (0, pad)))
    w_au = jnp.concatenate([w_a, w_u], axis=1).astype(BF16)
    cw = jnp.pad(conv_w, ((0, 0), (0, pad)))
    cb = jnp.pad(conv_b, ((0, pad),)).reshape(1, D_FF_PAD)
    w_down = (0.5 * jnp.pad(w_ffn_down, ((0, pad), (0, 0)))).astype(BF16)
    return w_qkv, w_lat, w_gate, w_uq_t, w_k, w_vt, w_au, cw, cb, w_down


def kernel(x_prompt, x_sample, c_prompt, c_sample, w_ada, b_ada, g_pre_mix, g_post_mix, w_in, rpb,
           g_q, w_uq, g_kv, w_ukv, w_na_proj, w_mla_proj, w_out, g_pre_ffn, g_post_ffn,
           w_ffn_in, conv_w, conv_b, w_ffn_down):
    assert w_ada.shape[0] == 1
    xs = x_sample.reshape(T_S, D)
    xp = x_prompt.reshape(T_P, D)
    c8 = jnp.concatenate([c_sample, c_prompt, jnp.zeros((8 - N_SEQ, D), F32)], axis=0)
    (w_qkv, w_lat, w_gate, w_uq_t, w_k, w_vt, w_au, cw, cb, w_down) = _prep_weights(
        w_in[0], w_uq[0], w_ukv[0], w_ffn_in[0], conv_w[0], conv_b[0], w_ffn_down[0])

    mod3 = _ada(c8, w_ada[0], b_ada[0]).reshape(8, N_MOD, D)

    h1 = _norm_mod(xs, xp, g_pre_mix[0], mod3)
    qkv = _matmul(h1, w_qkv, BF16, 1024, 1024, "in_proj_qkv")
    lat = _matmul(h1, w_lat, F32, 1024, 1024, "in_proj_lat")
    gates = _matmul(h1, w_gate, BF16, 1024, 1024, "in_proj_gate")

    bias = _na_bias(rpb[0])
    na_s = _na_attention(qkv, bias, 0, 1, L_S)
    na_p = _na_attention(qkv, bias, T_S, B_P, L_P)

    tabs, tabs_t = _rope_tables()
    qt = _uq(lat, g_q[0], w_uq_t, tabs_t)
    k, vt = _ukv(lat, g_kv[0], w_k, w_vt, tabs)
    mla_s = _mla_attention(qt, k, vt, 0, 1, L_S)
    mla_p = _mla_attention(qt, k, vt, T_S, B_P, L_P)

    merged = _merge(na_s, na_p, mla_s, mla_p,
                    w_na_proj[0].astype(BF16), w_mla_proj[0].astype(BF16), gates)
    y = _matmul(merged, w_out[0].astype(BF16), F32, 1024, 1024, "out_proj")
    x1, h2 = _resid_mix(y, xs, xp, mod3, g_post_mix[0], g_pre_ffn[0])

    au = _matmul(h2, w_au, BF16, 1024, 1024, "ffn_in")
    gated = _glu(au, cw, cb)
    f = _matmul_kacc(gated, w_down, 1024, 1024, D_FF_PAD // 4, "ffn_down")
    y_sample, y_prompt = _resid_ffn(f, x1, mod3, g_post_ffn[0])
    return (y_prompt.reshape(B_P, L_P, D), y_sample.reshape(1, L_S, D))
```

```python
import functools

import numpy as np
import jax
import jax.numpy as jnp
from jax import lax
from jax.experimental import pallas as pl
from jax.experimental.pallas import tpu as pltpu

F32 = jnp.float32
BF16 = jnp.bfloat16

D = 4096
B_P, L_P = 4, 2048
L_S = 16384
T_S = L_S
T_P = B_P * L_P
T = T_S + T_P
N_SEQ = 1 + B_P
N_MOD = 6
GRID_W = 64
NA_HEADS, NA_DIM = 16, 128
NA_WIDTH = NA_HEADS * NA_DIM
NA_KH, NA_KW = 8, 16
MLA_HEADS = 16
Q_LORA, KV_LORA = 896, 512
QK_NOPE, QK_ROPE, V_HEAD = 128, 64, 128
ROPE_THETA = 10000.0
D_FF = 11008
D_FF_PAD = 11264
EPS = 1e-6
NEG = -1e30
LOG2E = float(np.log2(np.e))

LAT_W = 2048
LAT_Q, LAT_KV, LAT_ROPE = 0, 1024, 1536

V7X_VMEM_BYTES = 64 * 1024 * 1024
VMEM_LIMIT = V7X_VMEM_BYTES - 8 * 1024 * 1024
LANES = 128
BF16_SUBLANES = 16

NA_QROWS = 8
NA_SLAB_ROWS = 16
NA_BQ = NA_QROWS * GRID_W
NA_BK = NA_SLAB_ROWS * GRID_W


def _params(n_axes):
    return pltpu.CompilerParams(dimension_semantics=("arbitrary",) * n_axes,
                                vmem_limit_bytes=VMEM_LIMIT)


def _seq_of_block(i, bm):
    t0 = i * bm
    return jnp.where(t0 < T_S, 0, 1 + jnp.maximum(t0 - T_S, 0) // L_P)


def _ada_kernel(c_ref, w_ref, b_ref, o_ref):
    c = c_ref[...]
    s = c * jax.nn.sigmoid(c)
    o_ref[...] = jnp.dot(s.astype(BF16), w_ref[...].astype(BF16),
                         preferred_element_type=F32) + b_ref[...]


def _ada(c8, w_ada, b_ada):
    bn = 512
    n = N_MOD * D
    return pl.pallas_call(
        _ada_kernel,
        grid=(n // bn,),
        in_specs=[pl.BlockSpec((8, D), lambda j: (0, 0)),
                  pl.BlockSpec((D, bn), lambda j: (0, j)),
                  pl.BlockSpec((1, bn), lambda j: (0, j))],
        out_specs=pl.BlockSpec((8, bn), lambda j: (0, j)),
        out_shape=jax.ShapeDtypeStruct((8, n), F32),
        compiler_params=_params(1),
        name="ada_mod",
    )(c8, w_ada, b_ada.reshape(1, n))


def _norm_mod_kernel(xs_ref, xp_ref, g_ref, mod_ref, o_ref, *, n_s):
    x = jnp.where(pl.program_id(0) < n_s, xs_ref[...], xp_ref[...])
    ms = jnp.mean(x * x, axis=-1, keepdims=True)
    y = x * lax.rsqrt(ms + EPS) * g_ref[...]
    o_ref[...] = (y * (1.0 + mod_ref[0, 1:2, :]) + mod_ref[0, 0:1, :]).astype(BF16)


def _norm_mod(xs, xp, g, mod3):
    bm = 256
    n_s = T_S // bm
    return pl.pallas_call(
        functools.partial(_norm_mod_kernel, n_s=n_s),
        grid=(T // bm,),
        in_specs=[pl.BlockSpec((bm, D), lambda i: (jnp.minimum(i, n_s - 1), 0)),
                  pl.BlockSpec((bm, D), lambda i: (jnp.maximum(i - n_s, 0), 0)),
                  pl.BlockSpec((1, D), lambda i: (0, 0)),
                  pl.BlockSpec((1, N_MOD, D), lambda i: (_seq_of_block(i, bm), 0, 0))],
        out_specs=pl.BlockSpec((bm, D), lambda i: (i, 0)),
        out_shape=jax.ShapeDtypeStruct((T, D), BF16),
        compiler_params=_params(1),
        name="norm_mod",
    )(xs, xp, g.reshape(1, D), mod3)


def _mm_kernel(a_ref, w_ref, o_ref):
    o_ref[...] = jnp.dot(a_ref[...], w_ref[...],
                         preferred_element_type=F32).astype(o_ref.dtype)


def _matmul(a, w, out_dtype, bm, bn, name):
    m, k = a.shape
    _, n = w.shape
    return pl.pallas_call(
        _mm_kernel,
        grid=(m // bm, n // bn),
        in_specs=[pl.BlockSpec((bm, k), lambda i, j: (i, 0)),
                  pl.BlockSpec((k, bn), lambda i, j: (0, j))],
        out_specs=pl.BlockSpec((bm, bn), lambda i, j: (i, j)),
        out_shape=jax.ShapeDtypeStruct((m, n), out_dtype),
        compiler_params=_params(2),
        name=name,
    )(a, w)


def _na_case_geometry(case, qr, kr):
    if case == 0:
        start = max(qr - NA_KH // 2, 0)
        return start <= kr < start + NA_KH, kr - qr + NA_KH - 1
    if case == 1:
        return qr <= kr < qr + NA_KH, kr - qr + NA_KH - 1 - NA_KH // 2
    lo = NA_QROWS + min(qr - NA_KH // 2, 0)
    return lo <= kr < lo + NA_KH, kr - qr - 1


def _na_bias_kernel(rpb_ref, o_ref):
    h = pl.program_id(0)
    n_dr, n_dc = 2 * NA_KH - 1, 2 * NA_KW - 1
    qc = lax.broadcasted_iota(jnp.int32, (GRID_W, LANES), 0)
    lane = lax.broadcasted_iota(jnp.int32, (GRID_W, LANES), 1)
    kc = lane % GRID_W
    cstart = jnp.clip(qc - NA_KW // 2, 0, GRID_W - NA_KW)
    col_valid = (kc >= cstart) & (kc < cstart + NA_KW)
    dc_idx = jnp.clip(kc - qc + NA_KW - 1, 0, n_dc - 1)
    neg = jnp.full((GRID_W, LANES), NEG, F32)
    tiles = []
    for i in range(n_dr):
        acc = jnp.zeros((GRID_W, LANES), F32)
        for j in range(n_dc):
            acc = jnp.where(dc_idx == j, rpb_ref[h * (n_dr * n_dc) + i * n_dc + j], acc)
        tiles.append(jnp.where(col_valid, acc * LOG2E, neg))
    left = lane < GRID_W
    for case in range(3):
        for qr in range(NA_QROWS):
            for kp in range(NA_SLAB_ROWS // 2):
                v0, i0 = _na_case_geometry(case, qr, 2 * kp)
                v1, i1 = _na_case_geometry(case, qr, 2 * kp + 1)
                t0 = tiles[i0] if v0 else neg
                t1 = tiles[i1] if v1 else neg
                if v0 or v1:
                    blk = jnp.where(left, t0, t1)
                else:
                    blk = neg
                o_ref[case, 0, qr * GRID_W:(qr + 1) * GRID_W,
                      kp * LANES:(kp + 1) * LANES] = blk


def _na_bias(rpb):
    for case in range(3):
        for qr in range(NA_QROWS):
            for kr in range(NA_SLAB_ROWS):
                valid, idx = _na_case_geometry(case, qr, kr)
                assert not valid or 0 <= idx < 2 * NA_KH - 1
    return pl.pallas_call(
        _na_bias_kernel,
        grid=(NA_HEADS,),
        in_specs=[pl.BlockSpec(memory_space=pltpu.SMEM)],
        out_specs=pl.BlockSpec((3, 1, NA_BQ, NA_BK), lambda h: (0, h, 0, 0)),
        out_shape=jax.ShapeDtypeStruct((3, NA_HEADS, NA_BQ, NA_BK), F32),
        compiler_params=_params(1),
        name="na_bias",
    )(rpb.reshape(-1))


def _na_kernel(q_ref, k_ref, v_ref, b0_ref, b1_ref, o_ref, *, n_rb, rows):
    pair = pl.program_id(1) % (n_rb // 2)
    for half, b_ref in enumerate((b0_ref, b1_ref)):
        rb = 2 * pair + half
        slab = jnp.clip(rb * NA_QROWS - NA_KH // 2, 0, rows - NA_SLAB_ROWS) * GRID_W
        slab = pl.multiple_of(slab, NA_KH // 2 * GRID_W)
        ks = k_ref[pl.ds(slab, NA_BK), :]
        vs = v_ref[pl.ds(slab, NA_BK), :]
        s = lax.dot_general(q_ref[half * NA_BQ:(half + 1) * NA_BQ, :], ks,
                            (((1,), (1,)), ((), ())), preferred_element_type=F32)
        s = s + b_ref[0, 0]
        m = jnp.max(s, axis=1, keepdims=True)
        p = jnp.exp2(s - m)
        l = jnp.sum(p, axis=1, keepdims=True)
        o = jnp.dot(p.astype(BF16), vs, preferred_element_type=F32)
        o_ref[half * NA_BQ:(half + 1) * NA_BQ, :] = (o / l).astype(BF16)


def _na_attention(qkv, bias, row_off, n_seq, seq_len):
    rows = seq_len // GRID_W
    n_rb = rows // NA_QROWS
    assert rows >= NA_SLAB_ROWS and row_off % seq_len == 0 and n_rb % 2 == 0
    n_pair = n_rb // 2
    q_off = row_off // (2 * NA_BQ)
    s_off = row_off // seq_len

    def case_of(i, half):
        rb = 2 * (i % n_pair) + half
        return jnp.where(rb == 0, 0, jnp.where(rb == n_rb - 1, 2, 1))

    return pl.pallas_call(
        functools.partial(_na_kernel, n_rb=n_rb, rows=rows),
        grid=(NA_HEADS, n_seq * n_pair),
        in_specs=[
            pl.BlockSpec((2 * NA_BQ, NA_DIM), lambda h, i: (q_off + i, h)),
            pl.BlockSpec((seq_len, NA_DIM), lambda h, i: (s_off + i // n_pair, NA_HEADS + h)),
            pl.BlockSpec((seq_len, NA_DIM), lambda h, i: (s_off + i // n_pair, 2 * NA_HEADS + h)),
            pl.BlockSpec((1, 1, NA_BQ, NA_BK), lambda h, i: (case_of(i, 0), h, 0, 0)),
            pl.BlockSpec((1, 1, NA_BQ, NA_BK), lambda h, i: (case_of(i, 1), h, 0, 0)),
        ],
        out_specs=pl.BlockSpec((2 * NA_BQ, NA_DIM), lambda h, i: (i, h)),
        out_shape=jax.ShapeDtypeStruct((n_seq * seq_len, NA_WIDTH), BF16),
        compiler_params=_params(2),
        name="na_attn",
    )(qkv, qkv, qkv, bias, bias)


def _rope_tables():
    half = QK_ROPE // 2
    inv = jnp.power(ROPE_THETA, -jnp.arange(0, QK_ROPE, 2, dtype=F32) / QK_ROPE)
    pos = jnp.concatenate([jnp.arange(L_S, dtype=F32),
                           jnp.tile(jnp.arange(L_P, dtype=F32), B_P)])
    ang = pos[:, None] * inv[None, :]
    cos, sin = jnp.cos(ang), jnp.sin(ang)
    z = jnp.zeros((T, LANES - QK_ROPE), F32)
    zh = jnp.zeros((T, half), F32)
    c_tab = jnp.concatenate([cos, cos, z], axis=1)
    sp_tab = jnp.concatenate([zh, sin, z], axis=1)
    sm_tab = jnp.concatenate([-sin, zh, z], axis=1)
    return (c_tab, sp_tab, sm_tab), (cos.T, sin.T)


def _apply_rope(r, c_ref, sp_ref, sm_ref):
    half = QK_ROPE // 2
    return (r * c_ref[...] + pltpu.roll(r, half, 1) * sp_ref[...]
            + pltpu.roll(r, LANES - half, 1) * sm_ref[...])


MLA_HW = 2 * LANES
MLA_BKV = 1024


def _rms_norm_bf16(x, g):
    ms = jnp.mean(x * x, axis=-1, keepdims=True)
    return (x * lax.rsqrt(ms + EPS) * g).astype(BF16)


def _uq_kernel(x_ref, g_ref, w_ref, cos_ref, sin_ref, o_ref):
    xn = _rms_norm_bf16(x_ref[...], g_ref[...])
    acc = lax.dot_general(w_ref[...], xn, (((1,), (1,)), ((), ())),
                          preferred_element_type=F32)
    scale = (QK_NOPE + QK_ROPE) ** -0.5 * LOG2E
    half = QK_ROPE // 2
    cos = cos_ref[...] * scale
    sin = sin_ref[...] * scale
    for h in range(MLA_HEADS):
        r0 = h * MLA_HW
        o_ref[r0:r0 + QK_NOPE, :] = (acc[r0:r0 + QK_NOPE, :] * scale).astype(BF16)
        x1 = acc[r0 + QK_NOPE:r0 + QK_NOPE + half, :]
        x2 = acc[r0 + QK_NOPE + half:r0 + QK_NOPE + QK_ROPE, :]
        o_ref[r0 + QK_NOPE:r0 + QK_NOPE + half, :] = (x1 * cos - x2 * sin).astype(BF16)
        o_ref[r0 + QK_NOPE + half:r0 + QK_NOPE + QK_ROPE, :] = (x1 * sin + x2 * cos).astype(BF16)
        o_ref[r0 + QK_NOPE + QK_ROPE:r0 + MLA_HW, :] = jnp.zeros(
            (MLA_HW - QK_NOPE - QK_ROPE, o_ref.shape[1]), BF16)


def _uq(lat, g_q, w_uq_t, tabs_t):
    bm = 512
    half = QK_ROPE // 2
    tab_spec = pl.BlockSpec((half, bm), lambda i: (0, i))
    return pl.pallas_call(
        _uq_kernel,
        grid=(T // bm,),
        in_specs=[pl.BlockSpec((bm, Q_LORA), lambda i: (i, LAT_Q // Q_LORA)),
                  pl.BlockSpec((1, Q_LORA), lambda i: (0, 0)),
                  pl.BlockSpec((MLA_HEADS * MLA_HW, Q_LORA), lambda i: (0, 0)),
                  tab_spec, tab_spec],
        out_specs=pl.BlockSpec((MLA_HEADS * MLA_HW, bm), lambda i: (0, i)),
        out_shape=jax.ShapeDtypeStruct((MLA_HEADS * MLA_HW, T), BF16),
        compiler_params=_params(1),
        name="mla_uq",
    )(lat, g_q.reshape(1, Q_LORA), w_uq_t, *tabs_t)


def _ukv_kernel(x_ref, r_ref, g_ref, wk_ref, wv_ref, c_ref, sp_ref, sm_ref, k_ref, vt_ref):
    xn = _rms_norm_bf16(x_ref[...], g_ref[...])
    kn = jnp.dot(xn, wk_ref[...], preferred_element_type=F32)
    kpe = _apply_rope(r_ref[...], c_ref, sp_ref, sm_ref).astype(BF16)
    vt = lax.dot_general(wv_ref[...], xn, (((1,), (1,)), ((), ())),
                         preferred_element_type=F32)
    for h in range(MLA_HEADS):
        k_ref[:, h * MLA_HW:h * MLA_HW + QK_NOPE] = kn[:, h * QK_NOPE:(h + 1) * QK_NOPE].astype(BF16)
        k_ref[:, h * MLA_HW + QK_NOPE:(h + 1) * MLA_HW] = kpe
        vt_ref[h, 0] = vt[h * V_HEAD:(h + 1) * V_HEAD, :].astype(BF16)


def _ukv(lat, g_kv, w_k, w_vt, tabs):
    bm = MLA_BKV
    tab_spec = pl.BlockSpec((bm, LANES), lambda i: (i, 0))
    return pl.pallas_call(
        _ukv_kernel,
        grid=(T // bm,),
        in_specs=[pl.BlockSpec((bm, KV_LORA), lambda i: (i, LAT_KV // KV_LORA)),
                  pl.BlockSpec((bm, LANES), lambda i: (i, LAT_ROPE // LANES)),
                  pl.BlockSpec((1, KV_LORA), lambda i: (0, 0)),
                  pl.BlockSpec((KV_LORA, MLA_HEADS * QK_NOPE), lambda i: (0, 0)),
                  pl.BlockSpec((MLA_HEADS * V_HEAD, KV_LORA), lambda i: (0, 0)),
                  tab_spec, tab_spec, tab_spec],
        out_specs=[pl.BlockSpec((bm, MLA_HEADS * MLA_HW), lambda i: (i, 0)),
                   pl.BlockSpec((MLA_HEADS, 1, V_HEAD, bm), lambda i: (0, i, 0, 0))],
        out_shape=[jax.ShapeDtypeStruct((T, MLA_HEADS * MLA_HW), BF16),
                   jax.ShapeDtypeStruct((MLA_HEADS, T // bm, V_HEAD, bm), BF16)],
        compiler_params=_params(1),
        name="mla_ukv",
    )(lat, lat, g_kv.reshape(1, KV_LORA), w_k, w_vt, *tabs)


def _flash_kernel(qt_ref, k_ref, vt_ref, o_ref, acc_ref, s_ref, *, nk):
    bq = qt_ref.shape[1]
    acc_ref[...] = jnp.zeros(acc_ref.shape, F32)

    def scores(j, slot):
        start = pl.multiple_of(j * MLA_BKV, MLA_BKV)
        s_ref[slot] = jnp.dot(k_ref[pl.ds(start, MLA_BKV), :], qt_ref[...],
                              preferred_element_type=F32)

    def update(j, slot, m_prev, l_prev):
        s = s_ref[slot]
        m_new = jnp.maximum(m_prev, jnp.max(s, axis=0, keepdims=True))
        alpha = jnp.exp2(m_prev - m_new)
        p = jnp.exp2(s - m_new)
        l_new = alpha * l_prev + jnp.sum(p, axis=0, keepdims=True)
        acc_ref[...] = alpha * acc_ref[...] + jnp.dot(
            vt_ref[0, j], p.astype(BF16), preferred_element_type=F32)
        return m_new, l_new

    def pair(jj, carry):
        m, l = carry
        j0 = 2 * jj
        scores(j0 + 1, 1)
        m, l = update(j0, 0, m, l)
        scores(jnp.minimum(j0 + 2, nk - 1), 0)
        m, l = update(j0 + 1, 1, m, l)
        return m, l

    scores(0, 0)
    m0 = jnp.full((1, bq), -jnp.inf, F32)
    l0 = jnp.zeros((1, bq), F32)
    _, l = lax.fori_loop(0, nk // 2, pair, (m0, l0))
    o_ref[...] = (acc_ref[...] / l).T.astype(BF16)


def _mla_attention(qt, k, vt, row_off, n_seq, seq_len, bq=1024):
    nq, nk = seq_len // bq, seq_len // MLA_BKV
    assert row_off % seq_len == 0 and nk % 2 == 0
    q_off, s_off = row_off // bq, row_off // seq_len
    return pl.pallas_call(
        functools.partial(_flash_kernel, nk=nk),
        grid=(n_seq, MLA_HEADS, nq),
        in_specs=[
            pl.BlockSpec((MLA_HW, bq), lambda b, h, qi: (h, q_off + b * nq + qi)),
            pl.BlockSpec((seq_len, MLA_HW), lambda b, h, qi: (s_off + b, h)),
            pl.BlockSpec((1, nk, V_HEAD, MLA_BKV), lambda b, h, qi: (h, s_off + b, 0, 0)),
        ],
        out_specs=pl.BlockSpec((bq, V_HEAD), lambda b, h, qi: (b * nq + qi, h)),
        out_shape=jax.ShapeDtypeStruct((n_seq * seq_len, MLA_HEADS * V_HEAD), BF16),
        scratch_shapes=[pltpu.VMEM((V_HEAD, bq), F32),
                        pltpu.VMEM((2, MLA_BKV, bq), F32)],
        compiler_params=_params(3),
        name="mla_flash",
    )(qt, k, vt)


def _merge_kernel(nas_ref, nap_ref, mlas_ref, mlap_ref, wna_ref, wmla_ref, gna_ref, gmla_ref,
                  o_ref, *, n_s):
    is_sample = pl.program_id(0) < n_s
    o_na = jnp.where(is_sample, nas_ref[...], nap_ref[...])
    o_mla = jnp.where(is_sample, mlas_ref[...], mlap_ref[...])
    pn = jnp.dot(o_na, wna_ref[...], preferred_element_type=F32)
    pm = jnp.dot(o_mla, wmla_ref[...], preferred_element_type=F32)
    o_ref[...] = (jax.nn.sigmoid(gna_ref[...].astype(F32)) * pn
                  + jax.nn.sigmoid(gmla_ref[...].astype(F32)) * pm).astype(BF16)


def _merge(na_s, na_p, mla_s, mla_p, w_na, w_mla, gates):
    bm, bn = 512, 1024
    n_s = T_S // bm
    width = NA_WIDTH
    assert width == MLA_HEADS * V_HEAD
    s_spec = pl.BlockSpec((bm, width), lambda i, j: (jnp.minimum(i, n_s - 1), 0))
    p_spec = pl.BlockSpec((bm, width), lambda i, j: (jnp.maximum(i - n_s, 0), 0))
    return pl.pallas_call(
        functools.partial(_merge_kernel, n_s=n_s),
        grid=(T // bm, D // bn),
        in_specs=[s_spec, p_spec, s_spec, p_spec,
                  pl.BlockSpec((width, bn), lambda i, j: (0, j)),
                  pl.BlockSpec((width, bn), lambda i, j: (0, j)),
                  pl.BlockSpec((bm, bn), lambda i, j: (i, j)),
                  pl.BlockSpec((bm, bn), lambda i, j: (i, D // bn + j))],
        out_specs=pl.BlockSpec((bm, bn), lambda i, j: (i, j)),
        out_shape=jax.ShapeDtypeStruct((T, D), BF16),
        compiler_params=_params(2),
        name="merge_proj",
    )(na_s, na_p, mla_s, mla_p, w_na, w_mla, gates, gates)


def _mm_kacc_kernel(a_ref, w_ref, o_ref):
    part = jnp.dot(a_ref[...], w_ref[...], preferred_element_type=F32)

    @pl.when(pl.program_id(2) == 0)
    def _():
        o_ref[...] = part

    @pl.when(pl.program_id(2) > 0)
    def _():
        o_ref[...] += part


def _matmul_kacc(a, w, bm, bn, bk, name):
    m, kdim = a.shape
    _, n = w.shape
    return pl.pallas_call(
        _mm_kacc_kernel,
        grid=(m // bm, n // bn, kdim // bk),
        in_specs=[pl.BlockSpec((bm, bk), lambda i, j, k: (i, k)),
                  pl.BlockSpec((bk, bn), lambda i, j, k: (k, j))],
        out_specs=pl.BlockSpec((bm, bn), lambda i, j, k: (i, j)),
        out_shape=jax.ShapeDtypeStruct((m, n), F32),
        compiler_params=_params(3),
        name=name,
    )(a, w)


RES_BM = 256
N_S_BLOCKS = T_S // RES_BM


def _gated_residual(y, x, mod_ref, g_ref, gate_idx):
    ms = jnp.mean(y * y, axis=-1, keepdims=True)
    n = y * lax.rsqrt(ms + EPS) * g_ref[...]
    return x + mod_ref[0, gate_idx:gate_idx + 1, :] * n


def _resid_mix_kernel(y_ref, xs_ref, xp_ref, mod_ref, gpost_ref, gpre_ref, x1_ref, h_ref):
    x = jnp.where(pl.program_id(0) < N_S_BLOCKS, xs_ref[...], xp_ref[...])
    x1 = _gated_residual(y_ref[...], x, mod_ref, gpost_ref, 2)
    x1_ref[...] = x1
    ms = jnp.mean(x1 * x1, axis=-1, keepdims=True)
    n = x1 * lax.rsqrt(ms + EPS) * gpre_ref[...]
    h_ref[...] = (n * (1.0 + mod_ref[0, 4:5, :]) + mod_ref[0, 3:4, :]).astype(BF16)


def _split_row_specs(bm):
    n_s = T_S // bm
    return (pl.BlockSpec((bm, D), lambda i: (jnp.minimum(i, n_s - 1), 0)),
            pl.BlockSpec((bm, D), lambda i: (jnp.maximum(i - n_s, 0), 0)))


def _resid_mix(y, xs, xp, mod3, g_post, g_pre):
    bm = RES_BM
    row = pl.BlockSpec((bm, D), lambda i: (i, 0))
    vec = pl.BlockSpec((1, D), lambda i: (0, 0))
    s_spec, p_spec = _split_row_specs(bm)
    return pl.pallas_call(
        _resid_mix_kernel,
        grid=(T // bm,),
        in_specs=[row, s_spec, p_spec,
                  pl.BlockSpec((1, N_MOD, D), lambda i: (_seq_of_block(i, bm), 0, 0)),
                  vec, vec],
        out_specs=[row, row],
        out_shape=[jax.ShapeDtypeStruct((T, D), F32), jax.ShapeDtypeStruct((T, D), BF16)],
        compiler_params=_params(1),
        name="resid_mix",
    )(y, xs, xp, mod3, g_post.reshape(1, D), g_pre.reshape(1, D))


def _resid_ffn_kernel(f_ref, x_ref, mod_ref, g_ref, os_ref, op_ref):
    out = _gated_residual(f_ref[...], x_ref[...], mod_ref, g_ref, 5)

    @pl.when(pl.program_id(0) < N_S_BLOCKS)
    def _():
        os_ref[...] = out

    @pl.when(pl.program_id(0) >= N_S_BLOCKS)
    def _():
        op_ref[...] = out


def _resid_ffn(f, x1, mod3, g_post):
    bm = RES_BM
    row = pl.BlockSpec((bm, D), lambda i: (i, 0))
    s_spec, p_spec = _split_row_specs(bm)
    return pl.pallas_call(
        _resid_ffn_kernel,
        grid=(T // bm,),
        in_specs=[row, row,
                  pl.BlockSpec((1, N_MOD, D), lambda i: (_seq_of_block(i, bm), 0, 0)),
                  pl.BlockSpec((1, D), lambda i: (0, 0))],
        out_specs=[s_spec, p_spec],
        out_shape=[jax.ShapeDtypeStruct((T_S, D), F32), jax.ShapeDtypeStruct((T_P, D), F32)],
        compiler_params=_params(1),
        name="resid_ffn",
    )(f, x1, mod3, g_post.reshape(1, D))


def _ffn_in_kernel(h_ref, hp_ref, hn_ref, wa_ref, wu_ref, cw_ref, cb_ref, o_ref, ext_ref, *, bm):
    halo = BF16_SUBLANES

    @pl.when(pl.program_id(1) == 0)
    def _():
        t0 = pl.program_id(0) * bm
        t1 = t0 + bm
        seq_start = (t0 == 0) | ((t0 >= T_S) & ((t0 - T_S) % L_P == 0))
        seq_end = (t1 == T_S) | ((t1 > T_S) & ((t1 - T_S) % L_P == 0))
        zero = jnp.zeros((halo, D), BF16)
        ext_ref[0:halo, :] = jnp.where(seq_start, zero, hp_ref[...])
        ext_ref[halo:halo + bm, :] = h_ref[...]
        ext_ref[halo + bm:, :] = jnp.where(seq_end, zero, hn_ref[...])

    a = jnp.dot(ext_ref[...], wa_ref[...], preferred_element_type=F32)
    u = jnp.dot(ext_ref[halo:halo + bm, :], wu_ref[...], preferred_element_type=F32)
    a_dn = pltpu.roll(a, 1, 0)[halo:halo + bm, :]
    a_up = pltpu.roll(a, bm + 2 * halo - 1, 0)[halo:halo + bm, :]
    x = (a_dn * cw_ref[0:1, :] + a[halo:halo + bm, :] * cw_ref[1:2, :]
         + a_up * cw_ref[2:3, :] + cb_ref[...])
    c = float(np.sqrt(2.0 / np.pi))
    t = jnp.tanh(x * (c + (c * 0.044715) * (x * x)))
    o_ref[...] = ((x + x * t) * u).astype(BF16)


def _ffn_in(h, w_au, conv_w, conv_b):
    bm, bn = 1024, 512
    nn = D_FF_PAD // bn
    halo = BF16_SUBLANES
    last_halo = T // halo - 1
    return pl.pallas_call(
        functools.partial(_ffn_in_kernel, bm=bm),
        grid=(T // bm, nn),
        in_specs=[
            pl.BlockSpec((bm, D), lambda i, j: (i, 0)),
            pl.BlockSpec((halo, D), lambda i, j: (jnp.maximum(i * (bm // halo) - 1, 0), 0)),
            pl.BlockSpec((halo, D), lambda i, j: (jnp.minimum((i + 1) * (bm // halo), last_halo), 0)),
            pl.BlockSpec((D, bn), lambda i, j: (0, j)),
            pl.BlockSpec((D, bn), lambda i, j: (0, nn + j)),
            pl.BlockSpec((3, bn), lambda i, j: (0, j)),
            pl.BlockSpec((1, bn), lambda i, j: (0, j)),
        ],
        out_specs=pl.BlockSpec((bm, bn), lambda i, j: (i, j)),
        out_shape=jax.ShapeDtypeStruct((T, D_FF_PAD), BF16),
        scratch_shapes=[pltpu.VMEM((bm + 2 * halo, D), BF16)],
        compiler_params=_params(2),
        name="ffn_in_glu",
    )(h, h, h, w_au, w_au, conv_w, conv_b)


def _prep_weights(w_in, w_uq, w_ukv, w_ffn_in, conv_w, conv_b, w_ffn_down):
    c_qkv = 3 * NA_WIDTH
    c_q = c_qkv + Q_LORA
    c_kv = c_q + KV_LORA
    c_r = c_kv + QK_ROPE
    q_scale = jnp.concatenate([jnp.full((NA_WIDTH,), NA_DIM ** -0.5 * LOG2E, F32),
                               jnp.ones((2 * NA_WIDTH,), F32)])
    w_qkv = (w_in[:, :c_qkv] * q_scale[None, :]).astype(BF16)
    w_lat = jnp.zeros((D, LAT_W), F32)
    w_lat = w_lat.at[:, LAT_Q:LAT_Q + Q_LORA].set(w_in[:, c_qkv:c_q])
    w_lat = w_lat.at[:, LAT_KV:LAT_KV + KV_LORA].set(w_in[:, c_q:c_kv])
    w_lat = w_lat.at[:, LAT_ROPE:LAT_ROPE + QK_ROPE].set(w_in[:, c_kv:c_r])
    w_lat = w_lat.astype(BF16)
    w_gate = w_in[:, c_r:].astype(BF16)
    uq = w_uq.reshape(Q_LORA, MLA_HEADS, QK_NOPE + QK_ROPE)
    uq = jnp.concatenate([uq, jnp.zeros((Q_LORA, MLA_HEADS, LANES - QK_ROPE), F32)], axis=2)
    w_uq_t = uq.reshape(Q_LORA, MLA_HEADS * MLA_HW).T.astype(BF16)
    ukv = w_ukv.reshape(KV_LORA, MLA_HEADS, QK_NOPE + V_HEAD)
    w_k = ukv[:, :, :QK_NOPE].reshape(KV_LORA, MLA_HEADS * QK_NOPE).astype(BF16)
    w_vt = ukv[:, :, QK_NOPE:].reshape(KV_LORA, MLA_HEADS * V_HEAD).T.astype(BF16)
    pad = D_FF_PAD - D_FF
    w_a = jnp.pad(w_ffn_in[:, :D_FF], ((0, 0), (0, pad)))
    w_u = jnp.pad(w_ffn_in[:, D_FF:], ((0, 0), (0, pad)))
    w_au = jnp.concatenate([w_a, w_u], axis=1).astype(BF16)
    cw = jnp.pad(conv_w, ((0, 0), (0, pad)))
    cb = jnp.pad(conv_b, ((0, pad),)).reshape(1, D_FF_PAD)
    w_down = (0.5 * jnp.pad(w_ffn_down, ((0, pad), (0, 0)))).astype(BF16)
    return w_qkv, w_lat, w_gate, w_uq_t, w_k, w_vt, w_au, cw, cb, w_down


def kernel(x_prompt, x_sample, c_prompt, c_sample, w_ada, b_ada, g_pre_mix, g_post_mix, w_in, rpb,
           g_q, w_uq, g_kv, w_ukv, w_na_proj, w_mla_proj, w_out, g_pre_ffn, g_post_ffn,
           w_ffn_in, conv_w, conv_b, w_ffn_down):
    assert w_ada.shape[0] == 1
    xs = x_sample.reshape(T_S, D)
    xp = x_prompt.reshape(T_P, D)
    c8 = jnp.concatenate([c_sample, c_prompt, jnp.zeros((8 - N_SEQ, D), F32)], axis=0)
    (w_qkv, w_lat, w_gate, w_uq_t, w_k, w_vt, w_au, cw, cb, w_down) = _prep_weights(
        w_in[0], w_uq[0], w_ukv[0], w_ffn_in[0], conv_w[0], conv_b[0], w_ffn_down[0])

    mod3 = _ada(c8, w_ada[0], b_ada[0]).reshape(8, N_MOD, D)

    h1 = _norm_mod(xs, xp, g_pre_mix[0], mod3)
    qkv = _matmul(h1, w_qkv, BF16, 1024, 1024, "in_proj_qkv")
    lat = _matmul(h1, w_lat, F32, 1024, 1024, "in_proj_lat")
    gates = _matmul(h1, w_gate, BF16, 1024, 1024, "in_proj_gate")

    bias = _na_bias(rpb[0])
    na_s = _na_attention(qkv, bias, 0, 1, L_S)
    na_p = _na_attention(qkv, bias, T_S, B_P, L_P)

    tabs, tabs_t = _rope_tables()
    qt = _uq(lat, g_q[0], w_uq_t, tabs_t)
    k, vt = _ukv(lat, g_kv[0], w_k, w_vt, tabs)
    mla_s = _mla_attention(qt, k, vt, 0, 1, L_S)
    mla_p = _mla_attention(qt, k, vt, T_S, B_P, L_P)

    merged = _merge(na_s, na_p, mla_s, mla_p,
                    w_na_proj[0].astype(BF16), w_mla_proj[0].astype(BF16), gates)
    y = _matmul(merged, w_out[0].astype(BF16), F32, 1024, 1024, "out_proj")
    x1, h2 = _resid_mix(y, xs, xp, mod3, g_post_mix[0], g_pre_ffn[0])

    gated = _ffn_in(h2, w_au, cw, cb)
    f = _matmul_kacc(gated, w_down, 1024, 1024, D_FF_PAD // 4, "ffn_down")
    y_sample, y_prompt = _resid_ffn(f, x1, mod3, g_post_ffn[0])
    return (y_prompt.reshape(B_P, L_P, D), y_sample.reshape(1, L_S, D))
```
